```python
import jax
import jax.numpy as jnp
from jax import lax
import numpy as np


D_MODEL = 2048
BATCH = 2
SEQ = 16384
DEPTH = 4

PLE_DIM = 256
D_FF = 4 * D_MODEL
N_EVEN = (DEPTH + 1) // 2
N_ODD = DEPTH // 2
NORM_EPS = 1e-6

POOL_WINDOWS = (2, 4, 8, 16)
POOL_GROUP = D_MODEL // 16
POOL_WIDTH = POOL_GROUP * len(POOL_WINDOWS)

GLA_HEADS = 4
GLA_WIDTH = D_MODEL - POOL_WIDTH
GLA_DV = GLA_WIDTH // GLA_HEADS
GLA_DK = GLA_DV // 2
GLA_QK = GLA_HEADS * GLA_DK
GLA_GATE_RANK = 16
GLA_TAU = 16.0
GLA_CHUNK = 64
EVEN_SPLITS = (POOL_WIDTH, POOL_WIDTH + GLA_QK, POOL_WIDTH + 2 * GLA_QK,
               POOL_WIDTH + 2 * GLA_QK + GLA_WIDTH, POOL_WIDTH + 2 * GLA_QK + 2 * GLA_WIDTH)
EVEN_IN = POOL_WIDTH + 2 * GLA_QK + 2 * GLA_WIDTH + GLA_GATE_RANK
EVEN_MIX = POOL_WIDTH + GLA_WIDTH

RWKV_HEAD = 64
RWKV_WIDTH = D_MODEL // 2
RWKV_HEADS = RWKV_WIDTH // RWKV_HEAD
RWKV_DECAY_RANK = 64
RWKV_A_RANK = 64
RWKV_GATE_RANK = 160
RWKV_SPLITS = (RWKV_WIDTH, 2 * RWKV_WIDTH, 3 * RWKV_WIDTH, 3 * RWKV_WIDTH + RWKV_DECAY_RANK,
               3 * RWKV_WIDTH + RWKV_DECAY_RANK + RWKV_A_RANK)
RWKV_IN = 3 * RWKV_WIDTH + RWKV_DECAY_RANK + RWKV_A_RANK + RWKV_GATE_RANK
RWKV_LN_EPS = 64e-5

DIL_WIDTH = D_MODEL - RWKV_WIDTH
DIL_HEAD = 128
DIL_HEADS = DIL_WIDTH // DIL_HEAD
DIL_PATTERNS = ((128, 1), (512, 4), (2048, 16))
DIL_BLOCK = 128
ROPE_THETA = 10000.0
ODD_IN = RWKV_IN + 3 * DIL_WIDTH
ODD_MIX = RWKV_WIDTH + DIL_WIDTH

kernel_name = 'hybrid_pool_gla_rwkv7_dilated_trunk'


def rmsnorm(x, g):
    xf = x.astype(jnp.float32)
    y = xf * lax.rsqrt(jnp.mean(xf * xf, axis=-1, keepdims=True) + NORM_EPS)
    return (y * g.astype(jnp.float32)).astype(x.dtype)


def shift_prev(h):
    return jnp.pad(h[:, :-1], ((0, 0), (1, 0), (0, 0)))


def pool_mixer(u, pool_w, pool_scale):
    S = u.shape[1]
    uf = u.astype(jnp.float32)
    cs = jnp.cumsum(uf, axis=1)
    outs = []
    for gi, w in enumerate(POOL_WINDOWS):
        lo, hi = gi * POOL_GROUP, (gi + 1) * POOL_GROUP
        c = cs[:, :, lo:hi]
        c_lag = jnp.pad(c, ((0, 0), (w, 0), (0, 0)))[:, :S]
        cnt = jnp.minimum(jnp.arange(1, S + 1), w).astype(jnp.float32)[None, :, None]
        pooled = (c - c_lag) / cnt - uf[:, :, lo:hi]
        outs.append(jnp.einsum('bsc,cd->bsd', pooled.astype(u.dtype), pool_w[gi]))
    return jnp.concatenate(outs, axis=-1) * pool_scale


def gla_chunked(q, k, v, gk):
    B, S, H, DK = q.shape
    DV = v.shape[-1]
    C = GLA_CHUNK
    n = S // C

    def to_chunks(t):
        return t.astype(jnp.float32).reshape(B, n, C, H, t.shape[-1]).transpose(1, 0, 3, 2, 4)

    qc = to_chunks(q) * (DK ** -0.5)
    kc, vc, gc = to_chunks(k), to_chunks(v), to_chunks(gk)
    causal = jnp.tril(jnp.ones((C, C), dtype=bool))

    def step(state, inp):
        qi, ki, vi, gi = inp
        b = jnp.cumsum(gi, axis=2)
        b_last = b[:, :, -1:, :]
        o_inter = jnp.einsum('bhcd,bhde->bhce', qi * jnp.exp(b), state)
        diff = jnp.where(causal[:, :, None], b[:, :, :, None, :] - b[:, :, None, :, :], -jnp.inf)
        att = jnp.einsum('bhid,bhjd,bhijd->bhij', qi, ki, jnp.exp(diff))
        o = o_inter + jnp.einsum('bhij,bhje->bhie', att, vi)
        state = state * jnp.exp(b_last)[:, :, 0, :, None] + jnp.einsum(
            'bhcd,bhce->bhde', ki * jnp.exp(b_last - b), vi)
        return state, o

    s0 = jnp.zeros((B, H, DK, DV), jnp.float32)
    _, o = lax.scan(step, s0, (qc, kc, vc, gc))
    return o.transpose(1, 0, 3, 2, 4).reshape(B, S, H, DV)


def rwkv7_time_mix(hc, mu, w0, w2, a0, a2, g2, k_k, k_a, r_k, ln_w, ln_b):
    B, S, _ = hc.shape
    f32 = jnp.float32
    hs = hc + (shift_prev(hc) - hc) * mu
    r, k, v, hw, ha, hg = jnp.split(hs, RWKV_SPLITS, axis=-1)
    w_log = -jax.nn.softplus(-(w0 + jnp.tanh(hw) @ w2).astype(f32)) - 0.5
    decay = jnp.exp(-jnp.exp(w_log))
    a = jax.nn.sigmoid((a0 + ha @ a2).astype(f32))
    g = jax.nn.sigmoid(hg) @ g2

    def heads(t):
        return t.astype(f32).reshape(B, S, RWKV_HEADS, RWKV_HEAD)

    kk = heads(k * k_k)
    kk = kk / jnp.maximum(jnp.linalg.norm(kk, axis=-1, keepdims=True), 1e-12)
    k_mod = k.astype(f32) * (1.0 + (a - 1.0) * k_a)
    rh, kh, vh, wh, ah = heads(r), heads(k_mod), heads(v), heads(decay), heads(a)
    a_vec = -kk
    b_vec = kk * ah

    def tm(t):
        return jnp.moveaxis(t, 1, 0)

    def step(state, inp):
        r_t, w_t, k_t, v_t, a_t, b_t = inp
        sa = jnp.einsum('bhvk,bhk->bhv', state, a_t)
        state = (state * w_t[:, :, None, :] + sa[..., None] * b_t[:, :, None, :]
                 + v_t[..., None] * k_t[:, :, None, :])
        y = jnp.einsum('bhvk,bhk->bhv', state, r_t)
        return state, y

    s0 = jnp.zeros((B, RWKV_HEADS, RWKV_HEAD, RWKV_HEAD), f32)
    _, y = lax.scan(step, s0, (tm(rh), tm(wh), tm(kh), tm(vh), tm(a_vec), tm(b_vec)))
    y = jnp.moveaxis(y, 0, 1)
    mean = jnp.mean(y, axis=-1, keepdims=True)
    var = jnp.mean(jnp.square(y - mean), axis=-1, keepdims=True)
    yn = ((y - mean) * lax.rsqrt(var + RWKV_LN_EPS)).reshape(B, S, RWKV_WIDTH) * ln_w + ln_b
    bonus = (jnp.sum(rh * kh * r_k, axis=-1, keepdims=True) * vh).reshape(B, S, RWKV_WIDTH)
    return ((yn + bonus) * g).astype(hc.dtype)


def rope(x, pos):
    half = x.shape[-1] // 2
    inv = ROPE_THETA ** (-jnp.arange(half, dtype=jnp.float32) / half)
    ang = pos.astype(jnp.float32)[:, None] * inv[None, :]
    cos = jnp.cos(ang)[None, :, None, :]
    sin = jnp.sin(ang)[None, :, None, :]
    xf = x.astype(jnp.float32)
    x1, x2 = xf[..., :half], xf[..., half:]
    return jnp.concatenate([x1 * cos - x2 * sin, x2 * cos + x1 * sin], axis=-1).astype(x.dtype)


def dilated_branch(q, k, v, window, dil):
    B, H, S, Dh = q.shape
    L = S // dil
    nb = -(-L // DIL_BLOCK)
    Lp = nb * DIL_BLOCK
    n_back = window // dil

    def sub(t):
        t = t.astype(jnp.float32).reshape(B, H, L, dil, Dh).transpose(0, 1, 3, 2, 4)
        t = jnp.pad(t, ((0, 0), (0, 0), (0, 0), (0, Lp - L), (0, 0)))
        return t.reshape(B, H, dil, nb, DIL_BLOCK, Dh)

    def with_prev(t):
        prev = jnp.pad(t, ((0, 0), (0, 0), (0, 0), (1, 0), (0, 0), (0, 0)))[:, :, :, :-1]
        return jnp.concatenate([prev, t], axis=4)

    qs = sub(q)
    kb, vb = with_prev(sub(k)), with_prev(sub(v))
    s = jnp.einsum('bhrnqd,bhrnkd->bhrnqk', qs, kb) * (Dh ** -0.5)
    qi = jnp.arange(DIL_BLOCK)[:, None]
    ki = jnp.arange(2 * DIL_BLOCK)[None, :]
    dist = qi + DIL_BLOCK - ki
    blk = jnp.arange(nb)[:, None, None]
    valid = (dist >= 0) & (dist <= n_back) & (blk * DIL_BLOCK + ki - DIL_BLOCK >= 0)
    s = jnp.where(valid, s, -jnp.inf)
    m = jnp.max(s, axis=-1)
    pexp = jnp.exp(s - m[..., None])
    l = jnp.sum(pexp, axis=-1)
    acc = jnp.einsum('bhrnqk,bhrnkd->bhrnqd', pexp, vb)

    def unsub(t):
        extra = t.shape[5:]
        t = t.reshape((B, H, dil, Lp) + extra)[:, :, :, :L]
        t = jnp.moveaxis(t, 2, 3)
        return t.reshape((B, H, S) + extra)

    return unsub(m), unsub(l), unsub(acc)


def dilated_attention(q, k, v):
    branches = [dilated_branch(q, k, v, w, d) for (w, d) in DIL_PATTERNS]
    m_max = jnp.max(jnp.stack([br[0] for br in branches]), axis=0)
    num = jnp.zeros(q.shape, jnp.float32)
    den = jnp.zeros(q.shape[:-1], jnp.float32)
    for m, l, acc in branches:
        c = jnp.exp(m - m_max)
        num = num + c[..., None] * acc
        den = den + c * l
    return num / den[..., None]


def even_mixer(h, w_in, w_out, pool_w, pool_scale, gate_w2, gate_b, gla_norm):
    B, S, _ = h.shape
    z = h @ w_in
    u, q, k, v, gout, glr = jnp.split(z, EVEN_SPLITS, axis=-1)
    a_out = pool_mixer(u, pool_w, pool_scale)
    gk = jax.nn.log_sigmoid((glr @ gate_w2 + gate_b).astype(jnp.float32)) / GLA_TAU
    o = gla_chunked(q.reshape(B, S, GLA_HEADS, GLA_DK), k.reshape(B, S, GLA_HEADS, GLA_DK),
                    v.reshape(B, S, GLA_HEADS, GLA_DV), gk.reshape(B, S, GLA_HEADS, GLA_DK))
    o = rmsnorm(o, gla_norm) * jax.nn.silu(gout.reshape(B, S, GLA_HEADS, GLA_DV).astype(jnp.float32))
    mix = jnp.concatenate([a_out.astype(h.dtype), o.reshape(B, S, GLA_WIDTH).astype(h.dtype)], axis=-1)
    return mix @ w_out


def odd_mixer(h, w_in, w_out, mu, w0, w2, a0, a2, g2, k_k, k_a, r_k, ln_w, ln_b):
    B, S, _ = h.shape
    z = h @ w_in
    hc, hd = z[..., :RWKV_IN], z[..., RWKV_IN:]
    c_out = rwkv7_time_mix(hc, mu, w0, w2, a0, a2, g2, k_k, k_a, r_k, ln_w, ln_b)
    q, k, v = jnp.split(hd, 3, axis=-1)
    pos = jnp.arange(S)
    q = rope(q.reshape(B, S, DIL_HEADS, DIL_HEAD), pos).transpose(0, 2, 1, 3)
    k = rope(k.reshape(B, S, DIL_HEADS, DIL_HEAD), pos).transpose(0, 2, 1, 3)
    v = v.reshape(B, S, DIL_HEADS, DIL_HEAD).transpose(0, 2, 1, 3)
    d_out = dilated_attention(q, k, v).transpose(0, 2, 1, 3).reshape(B, S, DIL_WIDTH)
    mix = jnp.concatenate([c_out.astype(h.dtype), d_out.astype(h.dtype)], axis=-1)
    return mix @ w_out


def setup_inputs(seed: int = 0) -> dict:
    key = jax.random.key(seed)
    ks = iter(jax.random.split(key, 40))
    f32 = jnp.float32

    def nrm(shape, scale):
        return jax.random.normal(next(ks), shape, f32) * scale

    def gain(shape):
        return 1.0 + 0.05 * jax.random.normal(next(ks), shape, f32)

    return {
        'x': nrm((BATCH, SEQ, D_MODEL), 1.0),
        'p': nrm((DEPTH, BATCH, SEQ, PLE_DIM), 1.0),
        'norm_mix_pre': gain((DEPTH, D_MODEL)),
        'norm_mix_post': gain((DEPTH, D_MODEL)),
        'norm_ffn_pre': gain((DEPTH, D_MODEL)),
        'norm_ffn_post': gain((DEPTH, D_MODEL)),
        'ev_w_in': nrm((N_EVEN, D_MODEL, EVEN_IN), D_MODEL ** -0.5),
        'ev_w_out': nrm((N_EVEN, EVEN_MIX, D_MODEL), EVEN_MIX ** -0.5),
        'pool_w': nrm((N_EVEN, len(POOL_WINDOWS), POOL_GROUP, POOL_GROUP), POOL_GROUP ** -0.5),
        'pool_scale': gain((N_EVEN, POOL_WIDTH)),
        'gla_gate_w2': nrm((N_EVEN, GLA_GATE_RANK, GLA_QK), GLA_GATE_RANK ** -0.5),
        'gla_gate_b': nrm((N_EVEN, GLA_QK), 0.1),
        'gla_norm': gain((N_EVEN, GLA_DV)),
        'od_w_in': nrm((N_ODD, D_MODEL, ODD_IN), D_MODEL ** -0.5),
        'od_w_out': nrm((N_ODD, ODD_MIX, D_MODEL), ODD_MIX ** -0.5),
        'rwkv_mu': jax.random.uniform(next(ks), (N_ODD, RWKV_IN), f32),
        'rwkv_w0': nrm((N_ODD, RWKV_WIDTH), 0.5) - 0.5,
        'rwkv_w2': nrm((N_ODD, RWKV_DECAY_RANK, RWKV_WIDTH), 0.5 * RWKV_DECAY_RANK ** -0.5),
        'rwkv_a0': nrm((N_ODD, RWKV_WIDTH), 0.1),
        'rwkv_a2': nrm((N_ODD, RWKV_A_RANK, RWKV_WIDTH), 0.5 * RWKV_A_RANK ** -0.5),
        'rwkv_g2': nrm((N_ODD, RWKV_GATE_RANK, RWKV_WIDTH), RWKV_GATE_RANK ** -0.5),
        'rwkv_k_k': 0.85 + nrm((N_ODD, RWKV_WIDTH), 0.05),
        'rwkv_k_a': gain((N_ODD, RWKV_WIDTH)),
        'rwkv_r_k': nrm((N_ODD, RWKV_HEADS, RWKV_HEAD), 0.1),
        'rwkv_ln_w': gain((N_ODD, RWKV_WIDTH)),
        'rwkv_ln_b': nrm((N_ODD, RWKV_WIDTH), 0.01),
        'ffn_up': nrm((DEPTH, D_MODEL, D_FF), D_MODEL ** -0.5),
        'ffn_down': nrm((DEPTH, D_FF, D_MODEL), D_FF ** -0.5),
        'ple_proj': nrm((DEPTH, PLE_DIM, D_MODEL), PLE_DIM ** -0.5),
        'ple_gate': nrm((DEPTH, D_MODEL, D_MODEL), D_MODEL ** -0.5),
        'ple_norm': gain((DEPTH, D_MODEL)),
    }


def reference(x, p, norm_mix_pre, norm_mix_post, norm_ffn_pre, norm_ffn_post,
              ev_w_in, ev_w_out, pool_w, pool_scale, gla_gate_w2, gla_gate_b, gla_norm,
              od_w_in, od_w_out, rwkv_mu, rwkv_w0, rwkv_w2, rwkv_a0, rwkv_a2, rwkv_g2,
              rwkv_k_k, rwkv_k_a, rwkv_r_k, rwkv_ln_w, rwkv_ln_b,
              ffn_up, ffn_down, ple_proj, ple_gate, ple_norm):
    for i in range(DEPTH):
        j = i // 2
        h = rmsnorm(x, norm_mix_pre[i])
        if i % 2 == 0:
            y = even_mixer(h, ev_w_in[j], ev_w_out[j], pool_w[j], pool_scale[j],
                           gla_gate_w2[j], gla_gate_b[j], gla_norm[j])
        else:
            y = odd_mixer(h, od_w_in[j], od_w_out[j], rwkv_mu[j], rwkv_w0[j], rwkv_w2[j],
                          rwkv_a0[j], rwkv_a2[j], rwkv_g2[j], rwkv_k_k[j], rwkv_k_a[j],
                          rwkv_r_k[j], rwkv_ln_w[j], rwkv_ln_b[j])
        x = x + rmsnorm(y, norm_mix_post[i])
        h = rmsnorm(x, norm_ffn_pre[i])
        y = jnp.square(jax.nn.relu(h @ ffn_up[i])) @ ffn_down[i]
        x = x + rmsnorm(y, norm_ffn_post[i])
        gate = jax.nn.sigmoid(rmsnorm(x, ple_norm[i]) @ ple_gate[i])
        x = x + (p[i] @ ple_proj[i]) * gate
    return x
```

```python
import functools
import math

import numpy as np
import jax
import jax.numpy as jnp
from jax import lax
from jax.experimental import pallas as pl
from jax.experimental.pallas import tpu as pltpu

F32 = jnp.float32
BF16 = jnp.bfloat16
HIGHEST = lax.Precision.HIGHEST

NORM_EPS = 1e-6
LANES = 128
VMEM_LIMIT_BYTES = 56 * 1024 * 1024

D_MODEL = 2048
PLE_DIM = 256
POOL_WINDOWS = (2, 4, 8, 16)
POOL_GROUP = 128
POOL_WIDTH = 512
POOL_HALO = 16
GLA_HEADS = 4
GLA_DK = 192
GLA_DV = 384
GLA_QK = GLA_HEADS * GLA_DK
GLA_WIDTH = GLA_HEADS * GLA_DV
GLA_GATE_RANK = 16
GLA_TAU = 16.0
RWKV_HEAD = 64
RWKV_WIDTH = 1024
RWKV_DECAY_RANK = 64
RWKV_A_RANK = 64
RWKV_GATE_RANK = 160
RWKV_LR = RWKV_DECAY_RANK + RWKV_A_RANK + RWKV_GATE_RANK
RWKV_LR_PAD = 384
RWKV_LN_EPS = 64e-5
DIL_WIDTH = 1024
DIL_HEAD = 128
DIL_HEADS = 8
DIL_PATTERNS = ((128, 1), (512, 4), (2048, 16))
DIL_BLOCK = 128
ROPE_THETA = 10000.0

PROJ_TM = 512
MIX_TM = 512
FFN_TM = 512
FFN_TF = 512
PLE_TM = 512
POOL_TS = 512
GLA_CHUNK = 64
GLA_TS = 512
RWKV_CHUNK = 64
RWKV_TS = 512
RWKV_PREP_TS = 512
DIL_TS = 2048


def _params(*sem):
    return pltpu.CompilerParams(dimension_semantics=sem, vmem_limit_bytes=VMEM_LIMIT_BYTES)


def _dot(a, b):
    return jnp.dot(a.astype(BF16), b.astype(BF16), preferred_element_type=F32)


def _dot_nt(a, b):
    return lax.dot_general(a.astype(BF16), b.astype(BF16), (((1,), (1,)), ((), ())),
                           preferred_element_type=F32)


def _dot_tn(a, b):
    return lax.dot_general(a.astype(BF16), b.astype(BF16), (((0,), (0,)), ((), ())),
                           preferred_element_type=F32)


def _dot_hi(a, b):
    return jnp.dot(a, b, precision=HIGHEST, preferred_element_type=F32)


def _dot_nt_hi(a, b):
    return lax.dot_general(a, b, (((1,), (1,)), ((), ())), precision=HIGHEST,
                           preferred_element_type=F32)


def _dot_sel(sel_bf16, x):
    hi = x.astype(BF16)
    r1 = x - hi.astype(F32)
    mid = r1.astype(BF16)
    lo = (r1 - mid.astype(F32)).astype(BF16)
    out = jnp.dot(sel_bf16, hi, preferred_element_type=F32)
    out = out + jnp.dot(sel_bf16, mid, preferred_element_type=F32)
    return out + jnp.dot(sel_bf16, lo, preferred_element_type=F32)


def _rms(x, gain):
    return x * lax.rsqrt(jnp.mean(x * x, axis=-1, keepdims=True) + NORM_EPS) * gain


def _sigmoid(x):
    return 1.0 / (1.0 + jnp.exp(-x))


def _log_sigmoid(x):
    return jnp.minimum(x, 0.0) - jnp.log1p(jnp.exp(-jnp.abs(x)))


def _proj_kernel(x_ref, gain_ref, w_ref, o_ref, h_ref):
    @pl.when(pl.program_id(1) == 0)
    def _():
        h_ref[...] = _rms(x_ref[...], gain_ref[...]).astype(BF16)

    o_ref[...] = jnp.dot(h_ref[...], w_ref[...], preferred_element_type=F32)


def _proj(x, gain, w):
    m, d = x.shape
    g, _, n = w.shape
    tm = min(PROJ_TM, m)
    return pl.pallas_call(
        _proj_kernel,
        grid=(m // tm, g),
        in_specs=[
            pl.BlockSpec((tm, d), lambda i, j: (i, 0)),
            pl.BlockSpec((1, d), lambda i, j: (0, 0)),
            pl.BlockSpec((None, d, n), lambda i, j: (j, 0, 0)),
        ],
        out_specs=pl.BlockSpec((None, tm, n), lambda i, j: (j, i, 0)),
        out_shape=jax.ShapeDtypeStruct((g, m, n), F32),
        scratch_shapes=[pltpu.VMEM((tm, d), BF16)],
        compiler_params=_params("parallel", "arbitrary"),
        name="norm_proj",
    )(x, gain, w)


def _mix_out_kernel(a1_ref, a2_ref, w1_ref, w2_ref, x_ref, gain_ref, o_ref):
    y = _dot(a1_ref[...], w1_ref[...]) + _dot(a2_ref[...], w2_ref[...])
    o_ref[...] = x_ref[...] + _rms(y, gain_ref[...])


def _mix_out(a1, a2, w1, w2, x, gain):
    m, d = x.shape
    k1, k2 = a1.shape[1], a2.shape[1]
    tm = min(MIX_TM, m)
    return pl.pallas_call(
        _mix_out_kernel,
        grid=(m // tm,),
        in_specs=[
            pl.BlockSpec((tm, k1), lambda i: (i, 0)),
            pl.BlockSpec((tm, k2), lambda i: (i, 0)),
            pl.BlockSpec((k1, d), lambda i: (0, 0)),
            pl.BlockSpec((k2, d), lambda i: (0, 0)),
            pl.BlockSpec((tm, d), lambda i: (i, 0)),
            pl.BlockSpec((1, d), lambda i: (0, 0)),
        ],
        out_specs=pl.BlockSpec((tm, d), lambda i: (i, 0)),
        out_shape=jax.ShapeDtypeStruct((m, d), F32),
        compiler_params=_params("parallel"),
        name="mix_out",
    )(a1, a2, w1, w2, x, gain)


def _ffn_kernel(x_ref, gpre_ref, gpost_ref, up_ref, down_ref, o_ref, h_ref, acc_ref):
    f = pl.program_id(1)

    @pl.when(f == 0)
    def _():
        h_ref[...] = _rms(x_ref[...], gpre_ref[...]).astype(BF16)
        acc_ref[...] = jnp.zeros_like(acc_ref)

    a = jnp.maximum(jnp.dot(h_ref[...], up_ref[...], preferred_element_type=F32), 0.0)
    acc_ref[...] += jnp.dot((a * a).astype(BF16), down_ref[...], preferred_element_type=F32)

    @pl.when(f == pl.num_programs(1) - 1)
    def _():
        o_ref[...] = x_ref[...] + _rms(acc_ref[...], gpost_ref[...])


def _ffn(x, gpre, gpost, up, down):
    m, d = x.shape
    dff = up.shape[1]
    tm = min(FFN_TM, m)
    tf = min(FFN_TF, dff)
    return pl.pallas_call(
        _ffn_kernel,
        grid=(m // tm, dff // tf),
        in_specs=[
            pl.BlockSpec((tm, d), lambda i, f: (i, 0)),
            pl.BlockSpec((1, d), lambda i, f: (0, 0)),
            pl.BlockSpec((1, d), lambda i, f: (0, 0)),
            pl.BlockSpec((d, tf), lambda i, f: (0, f)),
            pl.BlockSpec((tf, d), lambda i, f: (f, 0)),
        ],
        out_specs=pl.BlockSpec((tm, d), lambda i, f: (i, 0)),
        out_shape=jax.ShapeDtypeStruct((m, d), F32),
        scratch_shapes=[pltpu.VMEM((tm, d), BF16), pltpu.VMEM((tm, d), F32)],
        compiler_params=_params("parallel", "arbitrary"),
        name="ffn",
    )(x, gpre, gpost, up, down)


def _ple_kernel(x_ref, gain_ref, wg_ref, p_ref, wp_ref, o_ref):
    x = x_ref[...]
    gate = _sigmoid(_dot(_rms(x, gain_ref[...]), wg_ref[...]))
    o_ref[...] = x + _dot(p_ref[...], wp_ref[...]) * gate


def _ple(x, gain, wg, p, wp):
    m, d = x.shape
    pd = p.shape[1]
    tm = min(PLE_TM, m)
    return pl.pallas_call(
        _ple_kernel,
        grid=(m // tm,),
        in_specs=[
            pl.BlockSpec((tm, d), lambda i: (i, 0)),
            pl.BlockSpec((1, d), lambda i: (0, 0)),
            pl.BlockSpec((d, d), lambda i: (0, 0)),
            pl.BlockSpec((tm, pd), lambda i: (i, 0)),
            pl.BlockSpec((pd, d), lambda i: (0, 0)),
        ],
        out_specs=pl.BlockSpec((tm, d), lambda i: (i, 0)),
        out_shape=jax.ShapeDtypeStruct((m, d), F32),
        compiler_params=_params("parallel"),
        name="ple",
    )(x, gain, wg, p, wp)


def _pool_kernel(u_ref, prev_ref, w_ref, scale_ref, o_ref, ext_ref):
    i = pl.program_id(1)
    ts = u_ref.shape[0]
    ext_ref[pl.ds(POOL_HALO, ts), :] = u_ref[...]

    @pl.when(i == 0)
    def _():
        ext_ref[pl.ds(0, POOL_HALO), :] = jnp.zeros((POOL_HALO, POOL_WIDTH), F32)

    @pl.when(i > 0)
    def _():
        ext_ref[pl.ds(0, POOL_HALO), :] = prev_ref[...]

    t = i * ts + lax.broadcasted_iota(jnp.int32, (ts, 1), 0)
    for gi, w in enumerate(POOL_WINDOWS):
        lo = gi * POOL_GROUP
        cur = ext_ref[pl.ds(POOL_HALO, ts), pl.ds(lo, POOL_GROUP)]
        tot = cur
        for j in range(1, w):
            tot = tot + ext_ref[pl.ds(POOL_HALO - j, ts), pl.ds(lo, POOL_GROUP)]
        cnt = jnp.minimum(t + 1, w).astype(F32)
        pooled = tot / cnt - cur
        o_ref[:, pl.ds(lo, POOL_GROUP)] = _dot(pooled, w_ref[gi]) * scale_ref[:, pl.ds(lo, POOL_GROUP)]


def _pool(u, pool_w, pool_scale, batch):
    m, width = u.shape
    s = m // batch
    ts = min(POOL_TS, s)
    nt = s // ts
    hb = ts // POOL_HALO
    return pl.pallas_call(
        _pool_kernel,
        grid=(batch, nt),
        in_specs=[
            pl.BlockSpec((ts, width), lambda b, i: (b * nt + i, 0)),
            pl.BlockSpec((POOL_HALO, width), lambda b, i: (jnp.maximum((b * nt + i) * hb - 1, 0), 0)),
            pl.BlockSpec((len(POOL_WINDOWS), POOL_GROUP, POOL_GROUP), lambda b, i: (0, 0, 0)),
            pl.BlockSpec((1, width), lambda b, i: (0, 0)),
        ],
        out_specs=pl.BlockSpec((ts, width), lambda b, i: (b * nt + i, 0)),
        out_shape=jax.ShapeDtypeStruct((m, width), F32),
        scratch_shapes=[pltpu.VMEM((POOL_HALO + ts, width), F32)],
        compiler_params=_params("parallel", "arbitrary"),
        name="pool_mixer",
    )(u, u, pool_w, pool_scale)


def _gla_constants(c):
    nlev = int(math.log2(c))
    i = np.arange(c)[:, None]
    j = np.arange(c)[None, :]
    mats = [j <= i, j > i]
    for lev in range(nlev):
        s = 1 << lev
        blk = (i // (2 * s)) * (2 * s)
        second = ((i // s) % 2) == 1
        mq = second & (j >= blk + s) & (j <= i)
        mk = (~second) & (j > i) & (j <= blk + s - 1)
        mats.append(mq | mk)
    sel = np.concatenate(mats, axis=0).astype(np.float32)
    x = i ^ j
    level = np.full((c, c), -1, np.int32)
    for lev in range(nlev):
        level = np.where((x >> lev) == 1, lev, level)
    level = np.where(j > i, -1, level)
    level = np.where(i == j, nlev, level)
    return jnp.asarray(sel, BF16), jnp.asarray(level, jnp.int32)


def _gla_kernel(q_ref, k_ref, v_ref, go_ref, glr_ref, w2_ref, gb_ref, gn_ref, sel_ref, lev_ref,
                o_ref, st_ref):
    c = GLA_CHUNK
    nlev = int(math.log2(c))
    ts = q_ref.shape[0]

    @pl.when(pl.program_id(2) == 0)
    def _():
        st_ref[...] = jnp.zeros_like(st_ref)

    sel = sel_ref[...]
    level = lev_ref[...]
    w2 = w2_ref[...]
    gb = gb_ref[...]
    gn = gn_ref[...]

    def chunk(ci, carry):
        r0 = pl.multiple_of(ci * c, c)
        q = q_ref[pl.ds(r0, c), :] * (GLA_DK ** -0.5)
        k = k_ref[pl.ds(r0, c), :]
        v = v_ref[pl.ds(r0, c), :]
        g = _log_sigmoid(_dot_hi(glr_ref[pl.ds(r0, c), :], w2) + gb) / GLA_TAU
        e = jnp.exp(_dot_sel(sel, g))
        st = st_ref[...]
        o = _dot_nt(q * e[0:c], st)
        att = jnp.where(level == nlev, _dot_nt(q, k), 0.0)
        for lev in range(nlev):
            el = e[(2 + lev) * c:(3 + lev) * c]
            att = jnp.where(level == lev, _dot_nt(q * el, k * el), att)
        o = o + _dot(att, v)
        decay_all = e[c - 1:c]
        st_ref[...] = st * decay_all + _dot_tn(v, k * e[c:2 * c])
        o = o * lax.rsqrt(jnp.mean(o * o, axis=-1, keepdims=True) + NORM_EPS) * gn
        gate = go_ref[pl.ds(r0, c), :]
        o_ref[pl.ds(r0, c), :] = o * (gate * _sigmoid(gate))
        return carry

    lax.fori_loop(0, ts // c, chunk, 0)


def _gla(qk, vg, glr, w2p, gate_b, gla_norm, batch):
    m = qk.shape[1]
    s = m // batch
    ts = min(GLA_TS, s)
    nt = s // ts
    h = GLA_HEADS
    sel, level = _gla_constants(GLA_CHUNK)
    row = lambda b, hh, i: b * nt + i
    return pl.pallas_call(
        _gla_kernel,
        grid=(batch, h, nt),
        in_specs=[
            pl.BlockSpec((None, ts, GLA_DK), lambda b, hh, i: (hh, row(b, hh, i), 0)),
            pl.BlockSpec((None, ts, GLA_DK), lambda b, hh, i: (h + hh, row(b, hh, i), 0)),
            pl.BlockSpec((None, ts, GLA_DV), lambda b, hh, i: (hh, row(b, hh, i), 0)),
            pl.BlockSpec((None, ts, GLA_DV), lambda b, hh, i: (h + hh, row(b, hh, i), 0)),
            pl.BlockSpec((ts, LANES), lambda b, hh, i: (row(b, hh, i), 0)),
            pl.BlockSpec((None, LANES, GLA_DK), lambda b, hh, i: (hh, 0, 0)),
            pl.BlockSpec((None, 1, GLA_DK), lambda b, hh, i: (hh, 0, 0)),
            pl.BlockSpec((1, GLA_DV), lambda b, hh, i: (0, 0)),
            pl.BlockSpec(sel.shape, lambda b, hh, i: (0, 0)),
            pl.BlockSpec(level.shape, lambda b, hh, i: (0, 0)),
        ],
        out_specs=pl.BlockSpec((ts, GLA_DV), lambda b, hh, i: (row(b, hh, i), hh)),
        out_shape=jax.ShapeDtypeStruct((m, GLA_WIDTH), F32),
        scratch_shapes=[pltpu.VMEM((GLA_DV, GLA_DK), F32)],
        compiler_params=_params("parallel", "parallel", "arbitrary"),
        name="gla",
    )(qk, qk, vg, vg, glr, w2p, gate_b, gla_norm, sel, level)


def _head_sum_matrix():
    lane = np.arange(LANES)
    return jnp.asarray((lane[:, None] // RWKV_HEAD) == (lane[None, :] // RWKV_HEAD), BF16)


def _shifted(cur, prev8, first):
    prev_row = jnp.where(first, 0.0, prev8[7:8, :])
    rolled = pltpu.roll(cur, 1, 0)
    row = lax.broadcasted_iota(jnp.int32, cur.shape, 0)
    return jnp.where(row == 0, prev_row, rolled)


def _rwkv_prep_kernel(rkv_ref, rkvp_ref, lr_ref, lrp_ref, mu_ref, mulr_ref, w0_ref, w2_ref, a0_ref,
                      a2_ref, g2_ref, kk_ref, ka_ref, hs_ref,
                      r_out, lw_out, k_out, v_out, a_out, b_out, g_out):
    first = pl.program_id(1) == 0

    def mixed(cur, prev8, mu):
        return cur + (_shifted(cur, prev8, first) - cur) * mu

    r = mixed(rkv_ref[0], rkvp_ref[0], mu_ref[0])
    k = mixed(rkv_ref[1], rkvp_ref[1], mu_ref[1])
    v = mixed(rkv_ref[2], rkvp_ref[2], mu_ref[2])
    lr = mixed(lr_ref[...], lrp_ref[...], mulr_ref[...])

    w_log = _log_sigmoid(w0_ref[...] + _dot_hi(jnp.tanh(lr), w2_ref[...])) - 0.5
    lw = -jnp.exp(w_log)
    a = _sigmoid(a0_ref[...] + _dot_hi(lr, a2_ref[...]))
    g = _dot_hi(_sigmoid(lr), g2_ref[...])

    kk = k * kk_ref[...]
    norm = jnp.sqrt(_dot_hi(kk * kk, hs_ref[...].astype(F32)))
    kk = kk / jnp.maximum(norm, 1e-12)
    r_out[...] = r
    lw_out[...] = lw
    k_out[...] = k * (1.0 + (a - 1.0) * ka_ref[...])
    v_out[...] = v
    a_out[...] = -kk
    b_out[...] = kk * a
    g_out[...] = g


def _rwkv_prep(z6, lr, mu_rkv, mu_lr, w0, w2p, a0, a2p, g2p, k_k, k_a, batch):
    m = lr.shape[0]
    s = m // batch
    ts = min(RWKV_PREP_TS, s)
    nt = s // ts
    nb = RWKV_WIDTH // LANES
    hs = _head_sum_matrix()
    row = lambda b, i, j: b * nt + i
    prow = lambda b, i, j: jnp.maximum((b * nt + i) * (ts // 8) - 1, 0)
    colspec = pl.BlockSpec((ts, LANES), lambda b, i, j: (row(b, i, j), j))
    vecspec = pl.BlockSpec((1, LANES), lambda b, i, j: (0, j))
    lrw = pl.BlockSpec((RWKV_LR_PAD, LANES), lambda b, i, j: (0, j))
    out = jax.ShapeDtypeStruct((m, RWKV_WIDTH), F32)
    return pl.pallas_call(
        _rwkv_prep_kernel,
        grid=(batch, nt, nb),
        in_specs=[
            pl.BlockSpec((3, ts, LANES), lambda b, i, j: (0, row(b, i, j), j)),
            pl.BlockSpec((3, 8, LANES), lambda b, i, j: (0, prow(b, i, j), j)),
            pl.BlockSpec((ts, RWKV_LR_PAD), lambda b, i, j: (row(b, i, j), 0)),
            pl.BlockSpec((8, RWKV_LR_PAD), lambda b, i, j: (prow(b, i, j), 0)),
            pl.BlockSpec((3, 1, LANES), lambda b, i, j: (0, 0, j)),
            pl.BlockSpec((1, RWKV_LR_PAD), lambda b, i, j: (0, 0)),
            vecspec, lrw, vecspec, lrw, lrw, vecspec, vecspec,
            pl.BlockSpec((LANES, LANES), lambda b, i, j: (0, 0)),
        ],
        out_specs=[colspec] * 7,
        out_shape=[out] * 7,
        compiler_params=_params("parallel", "arbitrary", "arbitrary"),
        name="rwkv_prep",
    )(z6, z6, lr, lr, mu_rkv, mu_lr, w0, w2p, a0, a2p, g2p, k_k, k_a, hs)


def _rwkv_constants(c):
    i = np.arange(c)[:, None]
    j = np.arange(c)[None, :]
    tri = jnp.asarray(j <= i, BF16)
    i2 = np.arange(2 * c)[:, None] % c
    j2 = np.arange(2 * c)[None, :] % c
    strict = jnp.asarray(i2 > j2, F32)
    incl = jnp.asarray(i2 >= j2, F32)
    return tri, strict, incl


def _rwkv_scan_kernel(r_ref, lw_ref, k_ref, v_ref, a_ref, b_ref, g_ref, lnw_ref, lnb_ref, rk_ref,
                      tri_ref, strict_ref, incl_ref, hs_ref, o_ref,
                      state_ref, p_ref, gm_ref, rq_ref, y0_ref):
    c = RWKV_CHUNK
    nlev = int(math.log2(c))
    ts = r_ref.shape[0]
    nchunk = ts // c

    @pl.when(pl.program_id(2) == 0)
    def _():
        state_ref[...] = jnp.zeros_like(state_ref)

    lane = lax.broadcasted_iota(jnp.int32, (1, LANES), 1)
    head0 = lane < RWKV_HEAD
    tri = tri_ref[...]
    strict = strict_ref[...]
    incl = incl_ref[...]
    eye = (lax.broadcasted_iota(jnp.int32, (LANES, LANES), 0)
           == lax.broadcasted_iota(jnp.int32, (LANES, LANES), 1)).astype(F32)

    def stack(x):
        return jnp.concatenate([jnp.where(head0, x, 0.0), jnp.where(head0, 0.0, x)], axis=0)

    def local(ci, carry):
        r0 = pl.multiple_of(ci * c, c)
        lw = lw_ref[pl.ds(r0, c), :]
        r = r_ref[pl.ds(r0, c), :]
        k = k_ref[pl.ds(r0, c), :]
        v = v_ref[pl.ds(r0, c), :]
        a = a_ref[pl.ds(r0, c), :]
        b = b_ref[pl.ds(r0, c), :]
        cum = _dot_sel(tri, lw)
        last = cum[c - 1:c]
        grow = jnp.exp(-cum)
        fall = jnp.exp(last - cum)
        at2 = stack(a * jnp.exp(cum - lw))
        rt2 = stack(r * jnp.exp(cum))
        bt2 = stack(b * grow)
        kt2 = stack(k * grow)
        bh2 = stack(b * fall)
        kh2 = stack(k * fall)
        v2 = stack(v)
        big = _dot_nt_hi(jnp.concatenate([at2, rt2], axis=0), jnp.concatenate([bt2, kt2], axis=0))
        a_ab = big[0:2 * c, 0:2 * c] * strict
        a_ak = big[0:2 * c, 2 * c:4 * c] * strict
        a_rb = big[2 * c:4 * c, 0:2 * c] * incl
        a_rk = big[2 * c:4 * c, 2 * c:4 * c] * incl
        rhs = jnp.concatenate([at2, _dot_hi(a_ak, v2)], axis=1)
        n = a_ab
        for lev in range(nlev):
            rhs = rhs + _dot_hi(n, rhs)
            if lev + 1 < nlev:
                n = _dot_hi(n, n)
        w2 = rhs[:, 0:LANES]
        z2 = rhs[:, LANES:2 * LANES]
        gamma = jnp.exp(last)
        p_ref[ci] = eye * gamma + lax.dot_general(w2, bh2, (((0,), (0,)), ((), ())), precision=HIGHEST,
                                                  preferred_element_type=F32)
        gm_ref[ci] = lax.dot_general(jnp.concatenate([z2, v2], axis=0), jnp.concatenate([bh2, kh2], axis=0),
                                     (((0,), (0,)), ((), ())), precision=HIGHEST,
                                     preferred_element_type=F32)
        rq_ref[ci] = rt2 + _dot_hi(a_rb, w2)
        y0_ref[ci] = _dot_hi(a_rb, z2) + _dot_hi(a_rk, v2)
        return carry

    lax.fori_loop(0, nchunk, local, 0)

    def seq(ci, carry):
        r0 = pl.multiple_of(ci * c, c)
        st = state_ref[...]
        y2 = _dot_nt_hi(rq_ref[ci], st) + y0_ref[ci]
        o_ref[pl.ds(r0, c), :] = y2[0:c] + y2[c:2 * c]
        state_ref[...] = _dot_hi(st, p_ref[ci]) + gm_ref[ci]
        return carry

    lax.fori_loop(0, nchunk, seq, 0)

    hs = hs_ref[...].astype(F32)
    y = o_ref[...]
    mean = _dot_hi(y, hs) * (1.0 / RWKV_HEAD)
    yc = y - mean
    var = _dot_hi(yc * yc, hs) * (1.0 / RWKV_HEAD)
    yn = yc * lax.rsqrt(var + RWKV_LN_EPS) * lnw_ref[...] + lnb_ref[...]
    bonus = _dot_hi(r_ref[...] * k_ref[...] * rk_ref[...], hs) * v_ref[...]
    o_ref[...] = (yn + bonus) * g_ref[...]


def _rwkv_scan(r, lw, k, v, a, b, g, ln_w, ln_b, r_k, batch):
    m = r.shape[0]
    s = m // batch
    ts = min(RWKV_TS, s)
    nt = s // ts
    nb = RWKV_WIDTH // LANES
    c = RWKV_CHUNK
    nchunk = ts // c
    tri, strict, incl = _rwkv_constants(c)
    hs = _head_sum_matrix()
    colspec = pl.BlockSpec((ts, LANES), lambda bb, j, i: (bb * nt + i, j))
    vecspec = pl.BlockSpec((1, LANES), lambda bb, j, i: (0, j))
    const = lambda arr: pl.BlockSpec(arr.shape, lambda bb, j, i: (0,) * arr.ndim)
    return pl.pallas_call(
        _rwkv_scan_kernel,
        grid=(batch, nb, nt),
        in_specs=[colspec] * 7 + [vecspec] * 3 + [const(tri), const(strict), const(incl), const(hs)],
        out_specs=colspec,
        out_shape=jax.ShapeDtypeStruct((m, RWKV_WIDTH), F32),
        scratch_shapes=[
            pltpu.VMEM((LANES, LANES), F32),
            pltpu.VMEM((nchunk, LANES, LANES), F32),
            pltpu.VMEM((nchunk, LANES, LANES), F32),
            pltpu.VMEM((nchunk, 2 * c, LANES), F32),
            pltpu.VMEM((nchunk, 2 * c, LANES), F32),
        ],
        compiler_params=_params("parallel", "parallel", "arbitrary"),
        name="rwkv_scan",
    )(r, lw, k, v, a, b, g, ln_w, ln_b, r_k, tri, strict, incl, hs)


def _rope(x, cos2, sin2):
    return x * cos2 + pltpu.roll(x, DIL_HEAD // 2, 1) * sin2


def _dil_kernel(q_ref, kp_ref, kc_ref, vp_ref, vc_ref, cos_ref, sin_ref, cosp_ref, sinp_ref, o_ref,
                qs_ref, ks_ref, vs_ref, acc_ref, m_ref, l_ref):
    ts = q_ref.shape[0]
    i = pl.program_id(2)
    blk = DIL_BLOCK
    qs_ref[...] = _rope(q_ref[...], cos_ref[...], sin_ref[...]) * (DIL_HEAD ** -0.5)
    ks_ref[pl.ds(0, ts), :] = _rope(kp_ref[...], cosp_ref[...], sinp_ref[...])
    ks_ref[pl.ds(ts, ts), :] = _rope(kc_ref[...], cos_ref[...], sin_ref[...])
    vs_ref[pl.ds(0, ts), :] = vp_ref[...]
    vs_ref[pl.ds(ts, ts), :] = vc_ref[...]

    qi = lax.broadcasted_iota(jnp.int32, (blk, 2 * blk), 0)
    ki = lax.broadcasted_iota(jnp.int32, (blk, 2 * blk), 1)
    dist = qi + blk - ki
    band = (dist >= 0) & (dist <= blk)

    for pi, (window, dil) in enumerate(DIL_PATTERNS):
        assert window // dil == blk
        per_res = ts // (blk * dil)

        def unit(u, carry, pi=pi, dil=dil, per_res=per_res):
            res = u // per_res
            nblk = u % per_res
            qstart = res + nblk * (blk * dil)
            kstart = ts + qstart - blk * dil
            q = qs_ref[pl.ds(qstart, blk, stride=dil), :]
            k = ks_ref[pl.ds(kstart, 2 * blk, stride=dil), :]
            v = vs_ref[pl.ds(kstart, 2 * blk, stride=dil), :]
            s = _dot_nt(q, k)
            key_pos = i * ts + (qstart - blk * dil) + ki * dil
            s = jnp.where(band & (key_pos >= 0), s, -jnp.inf)
            mx = jnp.max(s, axis=-1, keepdims=True)
            p = jnp.exp(s - mx)
            acc_ref[pi, pl.ds(qstart, blk, stride=dil), :] = _dot(p, v)
            m_ref[pi, pl.ds(qstart, blk, stride=dil), :] = jnp.broadcast_to(mx, (blk, LANES))
            l_ref[pi, pl.ds(qstart, blk, stride=dil), :] = jnp.broadcast_to(
                jnp.sum(p, axis=-1, keepdims=True), (blk, LANES))
            return carry

        lax.fori_loop(0, ts // blk, unit, 0)

    m_all = jnp.maximum(jnp.maximum(m_ref[0], m_ref[1]), m_ref[2])
    num = jnp.zeros((ts, LANES), F32)
    den = jnp.zeros((ts, LANES), F32)
    for pi in range(len(DIL_PATTERNS)):
        cf = jnp.exp(m_ref[pi] - m_all)
        num = num + cf * acc_ref[pi]
        den = den + cf * l_ref[pi]
    o_ref[...] = num / den


def _dilated(z6, cos2, sin2, batch):
    m = z6.shape[1]
    s = m // batch
    ts = DIL_TS
    assert s % ts == 0
    nt = s // ts
    cur = lambda which: pl.BlockSpec((None, ts, DIL_HEAD), lambda b, h, i: (which, b * nt + i, h))
    prev = lambda which: pl.BlockSpec((None, ts, DIL_HEAD),
                                      lambda b, h, i: (which, b * nt + jnp.maximum(i - 1, 0), h))
    tab = pl.BlockSpec((ts, DIL_HEAD), lambda b, h, i: (i, 0))
    tabp = pl.BlockSpec((ts, DIL_HEAD), lambda b, h, i: (jnp.maximum(i - 1, 0), 0))
    npat = len(DIL_PATTERNS)
    return pl.pallas_call(
        _dil_kernel,
        grid=(batch, DIL_HEADS, nt),
        in_specs=[cur(3), prev(4), cur(4), prev(5), cur(5), tab, tab, tabp, tabp],
        out_specs=pl.BlockSpec((ts, DIL_HEAD), lambda b, h, i: (b * nt + i, h)),
        out_shape=jax.ShapeDtypeStruct((m, DIL_WIDTH), F32),
        scratch_shapes=[
            pltpu.VMEM((ts, DIL_HEAD), F32),
            pltpu.VMEM((2 * ts, DIL_HEAD), F32),
            pltpu.VMEM((2 * ts, DIL_HEAD), F32),
            pltpu.VMEM((npat, ts, DIL_HEAD), F32),
            pltpu.VMEM((npat, ts, LANES), F32),
            pltpu.VMEM((npat, ts, LANES), F32),
        ],
        compiler_params=_params("parallel", "parallel", "arbitrary"),
        name="dilated_attention",
    )(z6, z6, z6, z6, z6, cos2, sin2, cos2, sin2)


def _heads(w, n_heads, width):
    d = w.shape[0]
    return w.reshape(d, n_heads, width).transpose(1, 0, 2)


def _pad_rows(w, start, total):
    return jnp.zeros((total, w.shape[1]), w.dtype).at[start:start + w.shape[0]].set(w)


def _even_mixer(x, gain_pre, gain_post, w_in, w_out, pool_w, pool_scale, gate_w2, gate_b, gla_norm, batch):
    d = x.shape[1]
    o0 = POOL_WIDTH
    o1 = o0 + GLA_QK
    o2 = o1 + GLA_QK
    o3 = o2 + GLA_WIDTH
    o4 = o3 + GLA_WIDTH
    wb = w_in.astype(BF16)
    w_u = wb[:, :o0][None]
    w_qk = jnp.concatenate([_heads(wb[:, o0:o1], GLA_HEADS, GLA_DK), _heads(wb[:, o1:o2], GLA_HEADS, GLA_DK)], 0)
    w_vg = jnp.concatenate([_heads(wb[:, o2:o3], GLA_HEADS, GLA_DV), _heads(wb[:, o3:o4], GLA_HEADS, GLA_DV)], 0)
    w_glr = jnp.zeros((d, LANES), BF16).at[:, :GLA_GATE_RANK].set(wb[:, o4:])[None]
    u = _proj(x, gain_pre, w_u)[0]
    qk = _proj(x, gain_pre, w_qk)
    vg = _proj(x, gain_pre, w_vg)
    glr = _proj(x, gain_pre, w_glr)[0]
    a_out = _pool(u, pool_w.astype(BF16), pool_scale[None], batch)
    w2p = _heads(_pad_rows(gate_w2, 0, LANES), GLA_HEADS, GLA_DK)
    o = _gla(qk, vg, glr, w2p, gate_b.reshape(GLA_HEADS, 1, GLA_DK), gla_norm[None], batch)
    wo = w_out.astype(BF16)
    return _mix_out(a_out, o, wo[:POOL_WIDTH], wo[POOL_WIDTH:], x, gain_post)


def _odd_mixer(x, gain_pre, gain_post, w_in, w_out, mu, w0, w2, a0, a2, g2, k_k, k_a, r_k, ln_w, ln_b,
               cos2, sin2, batch):
    d = x.shape[1]
    wb = w_in.astype(BF16)
    c3 = 3 * RWKV_WIDTH
    rwkv_in = c3 + RWKV_LR
    w6 = jnp.concatenate([_heads(wb[:, :c3], 3, RWKV_WIDTH), _heads(wb[:, rwkv_in:], 3, DIL_WIDTH)], 0)
    w_lr = jnp.zeros((d, RWKV_LR_PAD), BF16).at[:, :RWKV_LR].set(wb[:, c3:rwkv_in])[None]
    z6 = _proj(x, gain_pre, w6)
    lr = _proj(x, gain_pre, w_lr)[0]
    mu_rkv = mu[:c3].reshape(3, 1, RWKV_WIDTH)
    mu_lr = jnp.zeros((1, RWKV_LR_PAD), F32).at[0, :RWKV_LR].set(mu[c3:])
    w2p = _pad_rows(w2, 0, RWKV_LR_PAD)
    a2p = _pad_rows(a2, RWKV_DECAY_RANK, RWKV_LR_PAD)
    g2p = _pad_rows(g2, RWKV_DECAY_RANK + RWKV_A_RANK, RWKV_LR_PAD)
    r, lw, k, v, a, b, g = _rwkv_prep(z6, lr, mu_rkv, mu_lr, w0[None], w2p, a0[None], a2p, g2p,
                                      k_k[None], k_a[None], batch)
    c_out = _rwkv_scan(r, lw, k, v, a, b, g, ln_w[None], ln_b[None], r_k.reshape(1, RWKV_WIDTH), batch)
    d_out = _dilated(z6, cos2, sin2, batch)
    wo = w_out.astype(BF16)
    return _mix_out(c_out, d_out, wo[:RWKV_WIDTH], wo[RWKV_WIDTH:], x, gain_post)


def _rope_tables(s):
    half = DIL_HEAD // 2
    inv = ROPE_THETA ** (-jnp.arange(half, dtype=F32) / half)
    ang = jnp.arange(s).astype(F32)[:, None] * inv[None, :]
    cos, sin = jnp.cos(ang), jnp.sin(ang)
    return jnp.concatenate([cos, cos], axis=-1), jnp.concatenate([-sin, sin], axis=-1)


def kernel(x, p, norm_mix_pre, norm_mix_post, norm_ffn_pre, norm_ffn_post, ev_w_in, ev_w_out, pool_w, pool_scale, gla_gate_w2, gla_gate_b, gla_norm, od_w_in, od_w_out, rwkv_mu, rwkv_w0, rwkv_w2, rwkv_a0, rwkv_a2, rwkv_g2, rwkv_k_k, rwkv_k_a, rwkv_r_k, rwkv_ln_w, rwkv_ln_b, ffn_up, ffn_down, ple_proj, ple_gate, ple_norm):
    batch, s, d = x.shape
    depth = p.shape[0]
    m = batch * s
    xf = x.reshape(m, d)
    cos2, sin2 = _rope_tables(s)
    for i in range(depth):
        j = i // 2
        if i % 2 == 0:
            xf = _even_mixer(xf, norm_mix_pre[i][None], norm_mix_post[i][None], ev_w_in[j], ev_w_out[j],
                             pool_w[j], pool_scale[j], gla_gate_w2[j], gla_gate_b[j], gla_norm[j], batch)
        else:
            xf = _odd_mixer(xf, norm_mix_pre[i][None], norm_mix_post[i][None], od_w_in[j], od_w_out[j],
                            rwkv_mu[j], rwkv_w0[j], rwkv_w2[j], rwkv_a0[j], rwkv_a2[j], rwkv_g2[j],
                            rwkv_k_k[j], rwkv_k_a[j], rwkv_r_k[j], rwkv_ln_w[j], rwkv_ln_b[j],
                            cos2, sin2, batch)
        xf = _ffn(xf, norm_ffn_pre[i][None], norm_ffn_post[i][None],
                  ffn_up[i].astype(BF16), ffn_down[i].astype(BF16))
        xf = _ple(xf, ple_norm[i][None], ple_gate[i].astype(BF16), p[i].reshape(m, PLE_DIM),
                  ple_proj[i].astype(BF16))
    return xf.reshape(batch, s, d)
```

```python
import functools
import math

import numpy as np
import jax
import jax.numpy as jnp
from jax import lax
from jax.experimental import pallas as pl
from jax.experimental.pallas import tpu as pltpu

F32 = jnp.float32
BF16 = jnp.bfloat16

NORM_EPS = 1e-6
LANES = 128
VMEM_LIMIT_BYTES = 56 * 1024 * 1024

D_MODEL = 2048
PLE_DIM = 256
POOL_WINDOWS = (2, 4, 8, 16)
POOL_GROUP = 128
POOL_WIDTH = 512
POOL_HALO = 16
GLA_HEADS = 4
GLA_DK = 192
GLA_DV = 384
GLA_QK = GLA_HEADS * GLA_DK
GLA_WIDTH = GLA_HEADS * GLA_DV
GLA_GATE_RANK = 16
GLA_TAU = 16.0
RWKV_HEAD = 64
RWKV_WIDTH = 1024
RWKV_DECAY_RANK = 64
RWKV_A_RANK = 64
RWKV_GATE_RANK = 160
RWKV_LR = RWKV_DECAY_RANK + RWKV_A_RANK + RWKV_GATE_RANK
RWKV_LR_PAD = 384
RWKV_LN_EPS = 64e-5
DIL_WIDTH = 1024
DIL_HEAD = 128
DIL_HEADS = 8
DIL_PATTERNS = ((128, 1), (512, 4), (2048, 16))
DIL_BLOCK = 128
ROPE_THETA = 10000.0

PROJ_TM = 512
MIX_TM = 512
FFN_TM = 512
FFN_TF = 512
PLE_TM = 512
POOL_TS = 512
GLA_CHUNK = 64
GLA_TS = 512
GLA_GROUP = 4
RWKV_CHUNK = 64
RWKV_TS = 512
RWKV_GROUP = 4
RWKV_PREP_TS = 512
DIL_TS = 2048


def _params(*sem):
    return pltpu.CompilerParams(dimension_semantics=sem, vmem_limit_bytes=VMEM_LIMIT_BYTES)


def _dot(a, b):
    return jnp.dot(a.astype(BF16), b.astype(BF16), preferred_element_type=F32)


def _dot_nt(a, b):
    return lax.dot_general(a.astype(BF16), b.astype(BF16), (((1,), (1,)), ((), ())),
                           preferred_element_type=F32)


def _dot_tn(a, b):
    return lax.dot_general(a.astype(BF16), b.astype(BF16), (((0,), (0,)), ((), ())),
                           preferred_element_type=F32)


def _split(x):
    hi = x.astype(BF16)
    return hi, (x - hi.astype(F32)).astype(BF16)


def _dot3_general(a, b, dims):
    ah, al = _split(a)
    bh, bl = _split(b)
    d = lambda u, v: lax.dot_general(u, v, (dims, ((), ())), preferred_element_type=F32)
    return d(ah, bh) + d(ah, bl) + d(al, bh)


def _dot3(a, b):
    return _dot3_general(a, b, ((1,), (0,)))


def _dot3_nt(a, b):
    return _dot3_general(a, b, ((1,), (1,)))


def _dot_sel(sel_bf16, x):
    hi = x.astype(BF16)
    r1 = x - hi.astype(F32)
    mid = r1.astype(BF16)
    lo = (r1 - mid.astype(F32)).astype(BF16)
    out = jnp.dot(sel_bf16, hi, preferred_element_type=F32)
    out = out + jnp.dot(sel_bf16, mid, preferred_element_type=F32)
    return out + jnp.dot(sel_bf16, lo, preferred_element_type=F32)


def _rms(x, gain):
    return x * lax.rsqrt(jnp.mean(x * x, axis=-1, keepdims=True) + NORM_EPS) * gain


def _sigmoid(x):
    return 1.0 / (1.0 + jnp.exp(-x))


def _log_sigmoid(x):
    return jnp.minimum(x, 0.0) - jnp.log1p(jnp.exp(-jnp.abs(x)))


def _proj_kernel(x_ref, gain_ref, w_ref, o_ref, h_ref):
    @pl.when(pl.program_id(1) == 0)
    def _():
        h_ref[...] = _rms(x_ref[...], gain_ref[...]).astype(BF16)

    o_ref[...] = jnp.dot(h_ref[...], w_ref[...], preferred_element_type=F32)


def _proj(x, gain, w):
    m, d = x.shape
    g, _, n = w.shape
    tm = min(PROJ_TM, m)
    return pl.pallas_call(
        _proj_kernel,
        grid=(m // tm, g),
        in_specs=[
            pl.BlockSpec((tm, d), lambda i, j: (i, 0)),
            pl.BlockSpec((1, d), lambda i, j: (0, 0)),
            pl.BlockSpec((None, d, n), lambda i, j: (j, 0, 0)),
        ],
        out_specs=pl.BlockSpec((None, tm, n), lambda i, j: (j, i, 0)),
        out_shape=jax.ShapeDtypeStruct((g, m, n), F32),
        scratch_shapes=[pltpu.VMEM((tm, d), BF16)],
        compiler_params=_params("parallel", "arbitrary"),
        name="norm_proj",
    )(x, gain, w)


def _mix_out_kernel(a1_ref, a2_ref, w1_ref, w2_ref, x_ref, gain_ref, o_ref):
    y = _dot(a1_ref[...], w1_ref[...]) + _dot(a2_ref[...], w2_ref[...])
    o_ref[...] = x_ref[...] + _rms(y, gain_ref[...])


def _mix_out(a1, a2, w1, w2, x, gain):
    m, d = x.shape
    k1, k2 = a1.shape[1], a2.shape[1]
    tm = min(MIX_TM, m)
    return pl.pallas_call(
        _mix_out_kernel,
        grid=(m // tm,),
        in_specs=[
            pl.BlockSpec((tm, k1), lambda i: (i, 0)),
            pl.BlockSpec((tm, k2), lambda i: (i, 0)),
            pl.BlockSpec((k1, d), lambda i: (0, 0)),
            pl.BlockSpec((k2, d), lambda i: (0, 0)),
            pl.BlockSpec((tm, d), lambda i: (i, 0)),
            pl.BlockSpec((1, d), lambda i: (0, 0)),
        ],
        out_specs=pl.BlockSpec((tm, d), lambda i: (i, 0)),
        out_shape=jax.ShapeDtypeStruct((m, d), F32),
        compiler_params=_params("parallel"),
        name="mix_out",
    )(a1, a2, w1, w2, x, gain)


def _ffn_kernel(x_ref, gpre_ref, gpost_ref, up_ref, down_ref, o_ref, h_ref, acc_ref):
    f = pl.program_id(1)

    @pl.when(f == 0)
    def _():
        h_ref[...] = _rms(x_ref[...], gpre_ref[...]).astype(BF16)
        acc_ref[...] = jnp.zeros_like(acc_ref)

    a = jnp.maximum(jnp.dot(h_ref[...], up_ref[...], preferred_element_type=F32), 0.0)
    acc_ref[...] += jnp.dot((a * a).astype(BF16), down_ref[...], preferred_element_type=F32)

    @pl.when(f == pl.num_programs(1) - 1)
    def _():
        o_ref[...] = x_ref[...] + _rms(acc_ref[...], gpost_ref[...])


def _ffn(x, gpre, gpost, up, down):
    m, d = x.shape
    dff = up.shape[1]
    tm = min(FFN_TM, m)
    tf = min(FFN_TF, dff)
    return pl.pallas_call(
        _ffn_kernel,
        grid=(m // tm, dff // tf),
        in_specs=[
            pl.BlockSpec((tm, d), lambda i, f: (i, 0)),
            pl.BlockSpec((1, d), lambda i, f: (0, 0)),
            pl.BlockSpec((1, d), lambda i, f: (0, 0)),
            pl.BlockSpec((d, tf), lambda i, f: (0, f)),
            pl.BlockSpec((tf, d), lambda i, f: (f, 0)),
        ],
        out_specs=pl.BlockSpec((tm, d), lambda i, f: (i, 0)),
        out_shape=jax.ShapeDtypeStruct((m, d), F32),
        scratch_shapes=[pltpu.VMEM((tm, d), BF16), pltpu.VMEM((tm, d), F32)],
        compiler_params=_params("parallel", "arbitrary"),
        name="ffn",
    )(x, gpre, gpost, up, down)


def _ple_kernel(x_ref, gain_ref, wg_ref, p_ref, wp_ref, o_ref):
    x = x_ref[...]
    gate = _sigmoid(_dot(_rms(x, gain_ref[...]), wg_ref[...]))
    o_ref[...] = x + _dot(p_ref[...], wp_ref[...]) * gate


def _ple(x, gain, wg, p, wp):
    m, d = x.shape
    pd = p.shape[1]
    tm = min(PLE_TM, m)
    return pl.pallas_call(
        _ple_kernel,
        grid=(m // tm,),
        in_specs=[
            pl.BlockSpec((tm, d), lambda i: (i, 0)),
            pl.BlockSpec((1, d), lambda i: (0, 0)),
            pl.BlockSpec((d, d), lambda i: (0, 0)),
            pl.BlockSpec((tm, pd), lambda i: (i, 0)),
            pl.BlockSpec((pd, d), lambda i: (0, 0)),
        ],
        out_specs=pl.BlockSpec((tm, d), lambda i: (i, 0)),
        out_shape=jax.ShapeDtypeStruct((m, d), F32),
        compiler_params=_params("parallel"),
        name="ple",
    )(x, gain, wg, p, wp)


def _pool_kernel(u_ref, prev_ref, w_ref, scale_ref, o_ref, ext_ref):
    i = pl.program_id(1)
    ts = u_ref.shape[0]
    ext_ref[pl.ds(POOL_HALO, ts), :] = u_ref[...]

    @pl.when(i == 0)
    def _():
        ext_ref[pl.ds(0, POOL_HALO), :] = jnp.zeros((POOL_HALO, POOL_WIDTH), F32)

    @pl.when(i > 0)
    def _():
        ext_ref[pl.ds(0, POOL_HALO), :] = prev_ref[...]

    t = i * ts + lax.broadcasted_iota(jnp.int32, (ts, 1), 0)
    for gi, w in enumerate(POOL_WINDOWS):
        lo = gi * POOL_GROUP
        cur = ext_ref[pl.ds(POOL_HALO, ts), pl.ds(lo, POOL_GROUP)]
        tot = cur
        for j in range(1, w):
            tot = tot + ext_ref[pl.ds(POOL_HALO - j, ts), pl.ds(lo, POOL_GROUP)]
        cnt = jnp.minimum(t + 1, w).astype(F32)
        pooled = tot / cnt - cur
        o_ref[:, pl.ds(lo, POOL_GROUP)] = _dot(pooled, w_ref[gi]) * scale_ref[:, pl.ds(lo, POOL_GROUP)]


def _pool(u, pool_w, pool_scale, batch):
    m, width = u.shape
    s = m // batch
    ts = min(POOL_TS, s)
    nt = s // ts
    hb = ts // POOL_HALO
    return pl.pallas_call(
        _pool_kernel,
        grid=(batch, nt),
        in_specs=[
            pl.BlockSpec((ts, width), lambda b, i: (b * nt + i, 0)),
            pl.BlockSpec((POOL_HALO, width), lambda b, i: (jnp.maximum((b * nt + i) * hb - 1, 0), 0)),
            pl.BlockSpec((len(POOL_WINDOWS), POOL_GROUP, POOL_GROUP), lambda b, i: (0, 0, 0)),
            pl.BlockSpec((1, width), lambda b, i: (0, 0)),
        ],
        out_specs=pl.BlockSpec((ts, width), lambda b, i: (b * nt + i, 0)),
        out_shape=jax.ShapeDtypeStruct((m, width), F32),
        scratch_shapes=[pltpu.VMEM((POOL_HALO + ts, width), F32)],
        compiler_params=_params("parallel", "arbitrary"),
        name="pool_mixer",
    )(u, u, pool_w, pool_scale)


def _gla_constants(c):
    nlev = int(math.log2(c))
    i = np.arange(c)[:, None]
    j = np.arange(c)[None, :]
    mats = [j <= i, j > i]
    for lev in range(nlev):
        s = 1 << lev
        blk = (i // (2 * s)) * (2 * s)
        second = ((i // s) % 2) == 1
        mq = second & (j >= blk + s) & (j <= i)
        mk = (~second) & (j > i) & (j <= blk + s - 1)
        mats.append(mq | mk)
    sel = np.concatenate(mats, axis=0).astype(np.float32)
    x = i ^ j
    level = np.full((c, c), -1, np.int32)
    for lev in range(nlev):
        level = np.where((x >> lev) == 1, lev, level)
    level = np.where(j > i, -1, level)
    level = np.where(i == j, nlev, level)
    return jnp.asarray(sel, BF16), jnp.asarray(level, jnp.int32)


def _gla_kernel(q_ref, k_ref, v_ref, go_ref, glr_ref, w2_ref, gb_ref, gn_ref, sel_ref, lev_ref,
                o_ref, st_ref):
    c = GLA_CHUNK
    nlev = int(math.log2(c))
    ts = q_ref.shape[0]

    @pl.when(pl.program_id(2) == 0)
    def _():
        st_ref[...] = jnp.zeros_like(st_ref)

    sel = sel_ref[...]
    level = lev_ref[...]
    w2 = w2_ref[...]
    gb = gb_ref[...]
    gn = gn_ref[...]

    g_all = _log_sigmoid(_dot3(glr_ref[...], w2) + gb) / GLA_TAU
    st = st_ref[...]
    for gi in range(ts // (c * GLA_GROUP)):
        ids = range(gi * GLA_GROUP, (gi + 1) * GLA_GROUP)
        q = {ci: q_ref[pl.ds(ci * c, c), :] * (GLA_DK ** -0.5) for ci in ids}
        k = {ci: k_ref[pl.ds(ci * c, c), :] for ci in ids}
        e = {ci: jnp.exp(_dot_sel(sel, g_all[ci * c:(ci + 1) * c])) for ci in ids}
        att = {ci: jnp.where(level == nlev, _dot_nt(q[ci], k[ci]), 0.0) for ci in ids}
        for lev in range(nlev):
            lo, hi = (2 + lev) * c, (3 + lev) * c
            att = {ci: jnp.where(level == lev, _dot_nt(q[ci] * e[ci][lo:hi], k[ci] * e[ci][lo:hi]), att[ci])
                   for ci in ids}
        intra = {ci: _dot(att[ci], v_ref[pl.ds(ci * c, c), :]) for ci in ids}
        qe = {ci: q[ci] * e[ci][0:c] for ci in ids}
        ke = {ci: k[ci] * e[ci][c:2 * c] for ci in ids}
        for ci in ids:
            o = _dot_nt(qe[ci], st) + intra[ci]
            st = st * e[ci][c - 1:c] + _dot_tn(v_ref[pl.ds(ci * c, c), :], ke[ci])
            o = o * lax.rsqrt(jnp.mean(o * o, axis=-1, keepdims=True) + NORM_EPS) * gn
            gate = go_ref[pl.ds(ci * c, c), :]
            o_ref[pl.ds(ci * c, c), :] = o * (gate * _sigmoid(gate))
    st_ref[...] = st


def _gla(qk, vg, glr, w2p, gate_b, gla_norm, batch):
    m = qk.shape[1]
    s = m // batch
    ts = min(GLA_TS, s)
    nt = s // ts
    h = GLA_HEADS
    sel, level = _gla_constants(GLA_CHUNK)
    row = lambda b, hh, i: b * nt + i
    return pl.pallas_call(
        _gla_kernel,
        grid=(batch, h, nt),
        in_specs=[
            pl.BlockSpec((None, ts, GLA_DK), lambda b, hh, i: (hh, row(b, hh, i), 0)),
            pl.BlockSpec((None, ts, GLA_DK), lambda b, hh, i: (h + hh, row(b, hh, i), 0)),
            pl.BlockSpec((None, ts, GLA_DV), lambda b, hh, i: (hh, row(b, hh, i), 0)),
            pl.BlockSpec((None, ts, GLA_DV), lambda b, hh, i: (h + hh, row(b, hh, i), 0)),
            pl.BlockSpec((ts, LANES), lambda b, hh, i: (row(b, hh, i), 0)),
            pl.BlockSpec((None, LANES, GLA_DK), lambda b, hh, i: (hh, 0, 0)),
            pl.BlockSpec((None, 1, GLA_DK), lambda b, hh, i: (hh, 0, 0)),
            pl.BlockSpec((1, GLA_DV), lambda b, hh, i: (0, 0)),
            pl.BlockSpec(sel.shape, lambda b, hh, i: (0, 0)),
            pl.BlockSpec(level.shape, lambda b, hh, i: (0, 0)),
        ],
        out_specs=pl.BlockSpec((ts, GLA_DV), lambda b, hh, i: (row(b, hh, i), hh)),
        out_shape=jax.ShapeDtypeStruct((m, GLA_WIDTH), F32),
        scratch_shapes=[pltpu.VMEM((GLA_DV, GLA_DK), F32)],
        compiler_params=_params("parallel", "parallel", "arbitrary"),
        name="gla",
    )(qk, qk, vg, vg, glr, w2p, gate_b, gla_norm, sel, level)


def _head_sum_matrix():
    lane = np.arange(LANES)
    return jnp.asarray((lane[:, None] // RWKV_HEAD) == (lane[None, :] // RWKV_HEAD), BF16)


def _shifted(cur, prev8, first):
    prev_row = jnp.where(first, 0.0, prev8[7:8, :])
    rolled = pltpu.roll(cur, 1, 0)
    row = lax.broadcasted_iota(jnp.int32, cur.shape, 0)
    return jnp.where(row == 0, prev_row, rolled)


def _rwkv_prep_kernel(rkv_ref, rkvp_ref, lr_ref, lrp_ref, mu_ref, mulr_ref, w0_ref, w2_ref, a0_ref,
                      a2_ref, g2_ref, kk_ref, ka_ref, hs_ref,
                      r_out, lw_out, k_out, v_out, a_out, b_out, g_out):
    first = pl.program_id(1) == 0

    def mixed(cur, prev8, mu):
        return cur + (_shifted(cur, prev8, first) - cur) * mu

    r = mixed(rkv_ref[0], rkvp_ref[0], mu_ref[0])
    k = mixed(rkv_ref[1], rkvp_ref[1], mu_ref[1])
    v = mixed(rkv_ref[2], rkvp_ref[2], mu_ref[2])
    lr = mixed(lr_ref[...], lrp_ref[...], mulr_ref[...])

    lr_wa = lr[:, 0:LANES]
    lr_g = lr[:, LANES:RWKV_LR_PAD]
    w_log = _log_sigmoid(w0_ref[...] + _dot(jnp.tanh(lr_wa), w2_ref[...])) - 0.5
    lw = -jnp.exp(w_log)
    a = _sigmoid(a0_ref[...] + _dot(lr_wa, a2_ref[...]))
    g = _dot(_sigmoid(lr_g), g2_ref[...])

    kk = k * kk_ref[...]
    norm = jnp.sqrt(_dot(kk * kk, hs_ref[...]))
    kk = kk / jnp.maximum(norm, 1e-12)
    r_out[...] = r
    lw_out[...] = lw
    k_out[...] = k * (1.0 + (a - 1.0) * ka_ref[...])
    v_out[...] = v
    a_out[...] = -kk
    b_out[...] = kk * a
    g_out[...] = g


def _rwkv_prep(z6, lr, mu_rkv, mu_lr, w0, w2p, a0, a2p, g2p, k_k, k_a, batch):
    m = lr.shape[0]
    s = m // batch
    ts = min(RWKV_PREP_TS, s)
    nt = s // ts
    nb = RWKV_WIDTH // LANES
    hs = _head_sum_matrix()
    row = lambda b, i, j: b * nt + i
    prow = lambda b, i, j: jnp.maximum((b * nt + i) * (ts // 8) - 1, 0)
    colspec = pl.BlockSpec((ts, LANES), lambda b, i, j: (row(b, i, j), j))
    vecspec = pl.BlockSpec((1, LANES), lambda b, i, j: (0, j))
    lrw = pl.BlockSpec((LANES, LANES), lambda b, i, j: (0, j))
    lrg = pl.BlockSpec((RWKV_LR_PAD - LANES, LANES), lambda b, i, j: (0, j))
    out = jax.ShapeDtypeStruct((m, RWKV_WIDTH), F32)
    return pl.pallas_call(
        _rwkv_prep_kernel,
        grid=(batch, nt, nb),
        in_specs=[
            pl.BlockSpec((3, ts, LANES), lambda b, i, j: (0, row(b, i, j), j)),
            pl.BlockSpec((3, 8, LANES), lambda b, i, j: (0, prow(b, i, j), j)),
            pl.BlockSpec((ts, RWKV_LR_PAD), lambda b, i, j: (row(b, i, j), 0)),
            pl.BlockSpec((8, RWKV_LR_PAD), lambda b, i, j: (prow(b, i, j), 0)),
            pl.BlockSpec((3, 1, LANES), lambda b, i, j: (0, 0, j)),
            pl.BlockSpec((1, RWKV_LR_PAD), lambda b, i, j: (0, 0)),
            vecspec, lrw, vecspec, lrw, lrg, vecspec, vecspec,
            pl.BlockSpec((LANES, LANES), lambda b, i, j: (0, 0)),
        ],
        out_specs=[colspec] * 7,
        out_shape=[out] * 7,
        compiler_params=_params("parallel", "arbitrary", "arbitrary"),
        name="rwkv_prep",
    )(z6, z6, lr, lr, mu_rkv, mu_lr, w0, w2p, a0, a2p, g2p, k_k, k_a, hs)


def _rwkv_constants(c):
    i = np.arange(c)[:, None]
    j = np.arange(c)[None, :]
    tri = jnp.asarray(j <= i, BF16)
    i2 = np.arange(2 * c)[:, None] % c
    j2 = np.arange(2 * c)[None, :] % c
    strict = jnp.asarray(i2 > j2, F32)
    incl = jnp.asarray(i2 >= j2, F32)
    return tri, strict, incl


def _rwkv_scan_kernel(r_ref, lw_ref, k_ref, v_ref, a_ref, b_ref, g_ref, lnw_ref, lnb_ref, rk_ref,
                      tri_ref, strict_ref, incl_ref, hs_ref, o_ref,
                      state_ref, p_ref, gm_ref, rq_ref, y0_ref):
    c = RWKV_CHUNK
    nlev = int(math.log2(c))
    ts = r_ref.shape[0]
    nchunk = ts // c

    @pl.when(pl.program_id(2) == 0)
    def _():
        state_ref[...] = jnp.zeros_like(state_ref)

    lane = lax.broadcasted_iota(jnp.int32, (1, LANES), 1)
    head0 = lane < RWKV_HEAD
    tri = tri_ref[...]
    strict = strict_ref[...]
    incl = incl_ref[...]
    eye = (lax.broadcasted_iota(jnp.int32, (LANES, LANES), 0)
           == lax.broadcasted_iota(jnp.int32, (LANES, LANES), 1)).astype(F32)

    def stack(x):
        return jnp.concatenate([jnp.where(head0, x, 0.0), jnp.where(head0, 0.0, x)], axis=0)

    grp = RWKV_GROUP
    each = range(grp)

    def local(gi):
        base = gi * grp
        rows = [pl.ds((base + t) * c, c) for t in each]
        lw = [lw_ref[rows[t], :] for t in each]
        cum = [_dot_sel(tri, lw[t]) for t in each]
        last = [cum[t][c - 1:c] for t in each]
        grow = [jnp.exp(-cum[t]) for t in each]
        fall = [jnp.exp(last[t] - cum[t]) for t in each]
        at2 = [stack(a_ref[rows[t], :] * jnp.exp(cum[t] - lw[t])) for t in each]
        rt2 = [stack(r_ref[rows[t], :] * jnp.exp(cum[t])) for t in each]
        bt2 = [stack(b_ref[rows[t], :] * grow[t]) for t in each]
        kt2 = [stack(k_ref[rows[t], :] * grow[t]) for t in each]
        bh2 = [stack(b_ref[rows[t], :] * fall[t]) for t in each]
        kh2 = [stack(k_ref[rows[t], :] * fall[t]) for t in each]
        v2 = [stack(v_ref[rows[t], :]) for t in each]
        big = [_dot_nt(jnp.concatenate([at2[t], rt2[t]], axis=0), jnp.concatenate([bt2[t], kt2[t]], axis=0))
               for t in each]
        a_ab = [big[t][0:2 * c, 0:2 * c] * strict for t in each]
        a_ak = [big[t][0:2 * c, 2 * c:4 * c] * strict for t in each]
        a_rb = [big[t][2 * c:4 * c, 0:2 * c] * incl for t in each]
        a_rk = [big[t][2 * c:4 * c, 2 * c:4 * c] * incl for t in each]
        rhs = [jnp.concatenate([at2[t], _dot(a_ak[t], v2[t])], axis=1) for t in each]
        n = a_ab
        for lev in range(nlev):
            rhs = [rhs[t] + _dot(n[t], rhs[t]) for t in each]
            if lev + 1 < nlev:
                n = [_dot(n[t], n[t]) for t in each]
        for t in each:
            w2 = rhs[t][:, 0:LANES]
            z2 = rhs[t][:, LANES:2 * LANES]
            gamma = jnp.exp(last[t])
            p_ref[base + t] = eye * gamma + _dot_tn(w2, bh2[t])
            gm_ref[base + t] = _dot_tn(jnp.concatenate([z2, v2[t]], axis=0),
                                       jnp.concatenate([bh2[t], kh2[t]], axis=0))
            rq_ref[base + t] = rt2[t] + _dot(a_rb[t], w2)
            y0_ref[base + t] = _dot(a_rb[t], z2) + _dot(a_rk[t], v2[t])

    st = state_ref[...]
    for gi in range(nchunk // grp):
        local(gi)
        for ci in range(gi * grp, (gi + 1) * grp):
            y2 = _dot3_nt(rq_ref[ci], st) + y0_ref[ci]
            o_ref[pl.ds(ci * c, c), :] = y2[0:c] + y2[c:2 * c]
            st = _dot3(st, p_ref[ci]) + gm_ref[ci]
    state_ref[...] = st

    hs = hs_ref[...]
    y = o_ref[...]
    mean = _dot(y, hs) * (1.0 / RWKV_HEAD)
    yc = y - mean
    var = _dot(yc * yc, hs) * (1.0 / RWKV_HEAD)
    yn = yc * lax.rsqrt(var + RWKV_LN_EPS) * lnw_ref[...] + lnb_ref[...]
    bonus = _dot(r_ref[...] * k_ref[...] * rk_ref[...], hs) * v_ref[...]
    o_ref[...] = (yn + bonus) * g_ref[...]


def _rwkv_scan(r, lw, k, v, a, b, g, ln_w, ln_b, r_k, batch):
    m = r.shape[0]
    s = m // batch
    ts = min(RWKV_TS, s)
    nt = s // ts
    nb = RWKV_WIDTH // LANES
    c = RWKV_CHUNK
    nchunk = ts // c
    tri, strict, incl = _rwkv_constants(c)
    hs = _head_sum_matrix()
    colspec = pl.BlockSpec((ts, LANES), lambda bb, j, i: (bb * nt + i, j))
    vecspec = pl.BlockSpec((1, LANES), lambda bb, j, i: (0, j))
    const = lambda arr: pl.BlockSpec(arr.shape, lambda bb, j, i: (0,) * arr.ndim)
    return pl.pallas_call(
        _rwkv_scan_kernel,
        grid=(batch, nb, nt),
        in_specs=[colspec] * 7 + [vecspec] * 3 + [const(tri), const(strict), const(incl), const(hs)],
        out_specs=colspec,
        out_shape=jax.ShapeDtypeStruct((m, RWKV_WIDTH), F32),
        scratch_shapes=[
            pltpu.VMEM((LANES, LANES), F32),
            pltpu.VMEM((nchunk, LANES, LANES), F32),
            pltpu.VMEM((nchunk, LANES, LANES), F32),
            pltpu.VMEM((nchunk, 2 * c, LANES), F32),
            pltpu.VMEM((nchunk, 2 * c, LANES), F32),
        ],
        compiler_params=_params("parallel", "parallel", "arbitrary"),
        name="rwkv_scan",
    )(r, lw, k, v, a, b, g, ln_w, ln_b, r_k, tri, strict, incl, hs)


def _rope(x, cos2, sin2):
    return x * cos2 + pltpu.roll(x, DIL_HEAD // 2, 1) * sin2


def _dil_kernel(q_ref, kp_ref, kc_ref, vp_ref, vc_ref, cos_ref, sin_ref, cosp_ref, sinp_ref, o_ref,
                qs_ref, ks_ref, vs_ref, acc_ref, m_ref, l_ref):
    ts = q_ref.shape[0]
    i = pl.program_id(2)
    blk = DIL_BLOCK
    qs_ref[...] = _rope(q_ref[...], cos_ref[...], sin_ref[...]) * (DIL_HEAD ** -0.5)
    ks_ref[pl.ds(0, ts), :] = _rope(kp_ref[...], cosp_ref[...], sinp_ref[...])
    ks_ref[pl.ds(ts, ts), :] = _rope(kc_ref[...], cos_ref[...], sin_ref[...])
    vs_ref[pl.ds(0, ts), :] = vp_ref[...]
    vs_ref[pl.ds(ts, ts), :] = vc_ref[...]

    qi = lax.broadcasted_iota(jnp.int32, (blk, 2 * blk), 0)
    ki = lax.broadcasted_iota(jnp.int32, (blk, 2 * blk), 1)
    dist = qi + blk - ki
    band = (dist >= 0) & (dist <= blk)

    for pi, (window, dil) in enumerate(DIL_PATTERNS):
        assert window // dil == blk
        per_res = ts // (blk * dil)

        def unit(u, carry, pi=pi, dil=dil, per_res=per_res):
            res = u // per_res
            nblk = u % per_res
            qstart = res + nblk * (blk * dil)
            kstart = ts + qstart - blk * dil
            q = qs_ref[pl.ds(qstart, blk, stride=dil), :]
            k = ks_ref[pl.ds(kstart, 2 * blk, stride=dil), :]
            v = vs_ref[pl.ds(kstart, 2 * blk, stride=dil), :]
            s = _dot_nt(q, k)
            key_pos = i * ts + (qstart - blk * dil) + ki * dil
            s = jnp.where(band & (key_pos >= 0), s, -jnp.inf)
            mx = jnp.max(s, axis=-1, keepdims=True)
            p = jnp.exp(s - mx)
            acc_ref[pi, pl.ds(qstart, blk, stride=dil), :] = _dot(p, v)
            m_ref[pi, pl.ds(qstart, blk, stride=dil), :] = jnp.broadcast_to(mx, (blk, LANES))
            l_ref[pi, pl.ds(qstart, blk, stride=dil), :] = jnp.broadcast_to(
                jnp.sum(p, axis=-1, keepdims=True), (blk, LANES))
            return carry

        lax.fori_loop(0, ts // blk, unit, 0, unroll=4)

    m_all = jnp.maximum(jnp.maximum(m_ref[0], m_ref[1]), m_ref[2])
    num = jnp.zeros((ts, LANES), F32)
    den = jnp.zeros((ts, LANES), F32)
    for pi in range(len(DIL_PATTERNS)):
        cf = jnp.exp(m_ref[pi] - m_all)
        num = num + cf * acc_ref[pi]
        den = den + cf * l_ref[pi]
    o_ref[...] = num / den


def _dilated(z6, cos2, sin2, batch):
    m = z6.shape[1]
    s = m // batch
    ts = DIL_TS
    assert s % ts == 0
    nt = s // ts
    cur = lambda which: pl.BlockSpec((None, ts, DIL_HEAD), lambda b, h, i: (which, b * nt + i, h))
    prev = lambda which: pl.BlockSpec((None, ts, DIL_HEAD),
                                      lambda b, h, i: (which, b * nt + jnp.maximum(i - 1, 0), h))
    tab = pl.BlockSpec((ts, DIL_HEAD), lambda b, h, i: (i, 0))
    tabp = pl.BlockSpec((ts, DIL_HEAD), lambda b, h, i: (jnp.maximum(i - 1, 0), 0))
    npat = len(DIL_PATTERNS)
    return pl.pallas_call(
        _dil_kernel,
        grid=(batch, DIL_HEADS, nt),
        in_specs=[cur(3), prev(4), cur(4), prev(5), cur(5), tab, tab, tabp, tabp],
        out_specs=pl.BlockSpec((ts, DIL_HEAD), lambda b, h, i: (b * nt + i, h)),
        out_shape=jax.ShapeDtypeStruct((m, DIL_WIDTH), F32),
        scratch_shapes=[
            pltpu.VMEM((ts, DIL_HEAD), F32),
            pltpu.VMEM((2 * ts, DIL_HEAD), F32),
            pltpu.VMEM((2 * ts, DIL_HEAD), F32),
            pltpu.VMEM((npat, ts, DIL_HEAD), F32),
            pltpu.VMEM((npat, ts, LANES), F32),
            pltpu.VMEM((npat, ts, LANES), F32),
        ],
        compiler_params=_params("parallel", "parallel", "arbitrary"),
        name="dilated_attention",
    )(z6, z6, z6, z6, z6, cos2, sin2, cos2, sin2)


def _heads(w, n_heads, width):
    d = w.shape[0]
    return w.reshape(d, n_heads, width).transpose(1, 0, 2)


def _pad_rows(w, start, total):
    return jnp.zeros((total, w.shape[1]), w.dtype).at[start:start + w.shape[0]].set(w)


def _even_mixer(x, gain_pre, gain_post, w_in, w_out, pool_w, pool_scale, gate_w2, gate_b, gla_norm, batch):
    d = x.shape[1]
    o0 = POOL_WIDTH
    o1 = o0 + GLA_QK
    o2 = o1 + GLA_QK
    o3 = o2 + GLA_WIDTH
    o4 = o3 + GLA_WIDTH
    wb = w_in.astype(BF16)
    w_u = wb[:, :o0][None]
    w_qk = jnp.concatenate([_heads(wb[:, o0:o1], GLA_HEADS, GLA_DK), _heads(wb[:, o1:o2], GLA_HEADS, GLA_DK)], 0)
    w_vg = jnp.concatenate([_heads(wb[:, o2:o3], GLA_HEADS, GLA_DV), _heads(wb[:, o3:o4], GLA_HEADS, GLA_DV)], 0)
    w_glr = jnp.zeros((d, LANES), BF16).at[:, :GLA_GATE_RANK].set(wb[:, o4:])[None]
    u = _proj(x, gain_pre, w_u)[0]
    qk = _proj(x, gain_pre, w_qk)
    vg = _proj(x, gain_pre, w_vg)
    glr = _proj(x, gain_pre, w_glr)[0]
    a_out = _pool(u, pool_w.astype(BF16), pool_scale[None], batch)
    w2p = _heads(_pad_rows(gate_w2, 0, LANES), GLA_HEADS, GLA_DK)
    o = _gla(qk, vg, glr, w2p, gate_b.reshape(GLA_HEADS, 1, GLA_DK), gla_norm[None], batch)
    wo = w_out.astype(BF16)
    return _mix_out(a_out, o, wo[:POOL_WIDTH], wo[POOL_WIDTH:], x, gain_post)


def _odd_mixer(x, gain_pre, gain_post, w_in, w_out, mu, w0, w2, a0, a2, g2, k_k, k_a, r_k, ln_w, ln_b,
               cos2, sin2, batch):
    d = x.shape[1]
    wb = w_in.astype(BF16)
    c3 = 3 * RWKV_WIDTH
    rwkv_in = c3 + RWKV_LR
    w6 = jnp.concatenate([_heads(wb[:, :c3], 3, RWKV_WIDTH), _heads(wb[:, rwkv_in:], 3, DIL_WIDTH)], 0)
    w_lr = jnp.zeros((d, RWKV_LR_PAD), BF16).at[:, :RWKV_LR].set(wb[:, c3:rwkv_in])[None]
    z6 = _proj(x, gain_pre, w6)
    lr = _proj(x, gain_pre, w_lr)[0]
    mu_rkv = mu[:c3].reshape(3, 1, RWKV_WIDTH)
    mu_lr = jnp.zeros((1, RWKV_LR_PAD), F32).at[0, :RWKV_LR].set(mu[c3:])
    w2p = _pad_rows(w2, 0, LANES).astype(BF16)
    a2p = _pad_rows(a2, RWKV_DECAY_RANK, LANES).astype(BF16)
    g2p = _pad_rows(g2, 0, RWKV_LR_PAD - LANES).astype(BF16)
    r, lw, k, v, a, b, g = _rwkv_prep(z6, lr, mu_rkv, mu_lr, w0[None], w2p, a0[None], a2p, g2p,
                                      k_k[None], k_a[None], batch)
    c_out = _rwkv_scan(r, lw, k, v, a, b, g, ln_w[None], ln_b[None], r_k.reshape(1, RWKV_WIDTH), batch)
    d_out = _dilated(z6, cos2, sin2, batch)
    wo = w_out.astype(BF16)
    return _mix_out(c_out, d_out, wo[:RWKV_WIDTH], wo[RWKV_WIDTH:], x, gain_post)


def _rope_tables(s):
    half = DIL_HEAD // 2
    inv = ROPE_THETA ** (-jnp.arange(half, dtype=F32) / half)
    ang = jnp.arange(s).astype(F32)[:, None] * inv[None, :]
    cos, sin = jnp.cos(ang), jnp.sin(ang)
    return jnp.concatenate([cos, cos], axis=-1), jnp.concatenate([-sin, sin], axis=-1)


def kernel(x, p, norm_mix_pre, norm_mix_post, norm_ffn_pre, norm_ffn_post, ev_w_in, ev_w_out, pool_w, pool_scale, gla_gate_w2, gla_gate_b, gla_norm, od_w_in, od_w_out, rwkv_mu, rwkv_w0, rwkv_w2, rwkv_a0, rwkv_a2, rwkv_g2, rwkv_k_k, rwkv_k_a, rwkv_r_k, rwkv_ln_w, rwkv_ln_b, ffn_up, ffn_down, ple_proj, ple_gate, ple_norm):
    batch, s, d = x.shape
    depth = p.shape[0]
    m = batch * s
    xf = x.reshape(m, d)
    cos2, sin2 = _rope_tables(s)
    for i in range(depth):
        j = i // 2
        if i % 2 == 0:
            xf = _even_mixer(xf, norm_mix_pre[i][None], norm_mix_post[i][None], ev_w_in[j], ev_w_out[j],
                             pool_w[j], pool_scale[j], gla_gate_w2[j], gla_gate_b[j], gla_norm[j], batch)
        else:
            xf = _odd_mixer(xf, norm_mix_pre[i][None], norm_mix_post[i][None], od_w_in[j], od_w_out[j],
                            rwkv_mu[j], rwkv_w0[j], rwkv_w2[j], rwkv_a0[j], rwkv_a2[j], rwkv_g2[j],
                            rwkv_k_k[j], rwkv_k_a[j], rwkv_r_k[j], rwkv_ln_w[j], rwkv_ln_b[j],
                            cos2, sin2, batch)
        xf = _ffn(xf, norm_ffn_pre[i][None], norm_ffn_post[i][None],
                  ffn_up[i].astype(BF16), ffn_down[i].astype(BF16))
        xf = _ple(xf, ple_norm[i][None], ple_gate[i].astype(BF16), p[i].reshape(m, PLE_DIM),
                  ple_proj[i].astype(BF16))
    return xf.reshape(batch, s, d)
```

```python
import functools
import math

import numpy as np
import jax
import jax.numpy as jnp
from jax import lax
from jax.experimental import pallas as pl
from jax.experimental.pallas import tpu as pltpu

F32 = jnp.float32
BF16 = jnp.bfloat16

NORM_EPS = 1e-6
LANES = 128
VMEM_LIMIT_BYTES = 56 * 1024 * 1024

D_MODEL = 2048
PLE_DIM = 256
POOL_WINDOWS = (2, 4, 8, 16)
POOL_GROUP = 128
POOL_WIDTH = 512
POOL_HALO = 16
GLA_HEADS = 4
GLA_DK = 192
GLA_DV = 384
GLA_QK = GLA_HEADS * GLA_DK
GLA_WIDTH = GLA_HEADS * GLA_DV
GLA_GATE_RANK = 16
GLA_TAU = 16.0
RWKV_HEAD = 64
RWKV_WIDTH = 1024
RWKV_DECAY_RANK = 64
RWKV_A_RANK = 64
RWKV_GATE_RANK = 160
RWKV_LR = RWKV_DECAY_RANK + RWKV_A_RANK + RWKV_GATE_RANK
RWKV_LR_PAD = 384
RWKV_LN_EPS = 64e-5
DIL_WIDTH = 1024
DIL_HEAD = 128
DIL_HEADS = 8
DIL_PATTERNS = ((128, 1), (512, 4), (2048, 16))
DIL_BLOCK = 128
ROPE_THETA = 10000.0

PROJ_TM = 1024
MIX_TM = 512
FFN_TM = 512
FFN_TF = 1024
PLE_TM = 512
POOL_TS = 512
GLA_CHUNK = 64
GLA_TS = 512
GLA_GROUP = 8
RWKV_CHUNK = 64
RWKV_TS = 512
RWKV_GROUP = 8
DIL_TS = 2048


def _params(*sem):
    return pltpu.CompilerParams(dimension_semantics=sem, vmem_limit_bytes=VMEM_LIMIT_BYTES)


def _dot(a, b):
    return jnp.dot(a.astype(BF16), b.astype(BF16), preferred_element_type=F32)


def _dot_nt(a, b):
    return lax.dot_general(a.astype(BF16), b.astype(BF16), (((1,), (1,)), ((), ())),
                           preferred_element_type=F32)


def _dot_tn(a, b):
    return lax.dot_general(a.astype(BF16), b.astype(BF16), (((0,), (0,)), ((), ())),
                           preferred_element_type=F32)


def _split(x):
    hi = x.astype(BF16)
    return hi, (x - hi.astype(F32)).astype(BF16)


def _dot3_general(a, b, dims):
    ah, al = _split(a)
    bh, bl = _split(b)
    d = lambda u, v: lax.dot_general(u, v, (dims, ((), ())), preferred_element_type=F32)
    return d(ah, bh) + d(ah, bl) + d(al, bh)


def _dot3(a, b):
    return _dot3_general(a, b, ((1,), (0,)))


def _dot3_nt(a, b):
    return _dot3_general(a, b, ((1,), (1,)))


def _dot_sel(sel_bf16, x):
    hi = x.astype(BF16)
    r1 = x - hi.astype(F32)
    mid = r1.astype(BF16)
    lo = (r1 - mid.astype(F32)).astype(BF16)
    out = jnp.dot(sel_bf16, hi, preferred_element_type=F32)
    out = out + jnp.dot(sel_bf16, mid, preferred_element_type=F32)
    return out + jnp.dot(sel_bf16, lo, preferred_element_type=F32)


def _rms(x, gain):
    return x * lax.rsqrt(jnp.mean(x * x, axis=-1, keepdims=True) + NORM_EPS) * gain


def _sigmoid(x):
    return 1.0 / (1.0 + jnp.exp(-x))


def _log_sigmoid(x):
    return jnp.minimum(x, 0.0) - jnp.log1p(jnp.exp(-jnp.abs(x)))


def _proj_kernel(starts, x_ref, gain_ref, *refs):
    nw = len(starts) - 1
    w_refs, o_refs, h_ref = refs[:nw], refs[nw:2 * nw], refs[2 * nw]
    j = pl.program_id(1)

    @pl.when(j == 0)
    def _():
        h_ref[...] = _rms(x_ref[...], gain_ref[...]).astype(BF16)

    for k in range(nw):
        @pl.when((j >= starts[k]) & (j < starts[k + 1]))
        def _(k=k):
            o_refs[k][...] = jnp.dot(h_ref[...], w_refs[k][...], preferred_element_type=F32)


def _proj(x, gain, ws):
    m, d = x.shape
    tm = min(PROJ_TM, m)
    starts = [0]
    for w in ws:
        starts.append(starts[-1] + w.shape[0])

    def group(k):
        return lambda i, j: jnp.clip(j - starts[k], 0, ws[k].shape[0] - 1)

    w_specs = [pl.BlockSpec((None, d, w.shape[2]), lambda i, j, gk=group(k): (gk(i, j), 0, 0))
               for k, w in enumerate(ws)]
    o_specs = [pl.BlockSpec((None, tm, w.shape[2]), lambda i, j, gk=group(k): (gk(i, j), i, 0))
               for k, w in enumerate(ws)]
    return pl.pallas_call(
        functools.partial(_proj_kernel, tuple(starts)),
        grid=(m // tm, starts[-1]),
        in_specs=[pl.BlockSpec((tm, d), lambda i, j: (i, 0)), pl.BlockSpec((1, d), lambda i, j: (0, 0))] + w_specs,
        out_specs=o_specs,
        out_shape=[jax.ShapeDtypeStruct((w.shape[0], m, w.shape[2]), F32) for w in ws],
        scratch_shapes=[pltpu.VMEM((tm, d), BF16)],
        compiler_params=_params("parallel", "arbitrary"),
        name="norm_proj",
    )(x, gain, *ws)


def _mix_out_kernel(a1_ref, a2_ref, w1_ref, w2_ref, x_ref, gain_ref, o_ref):
    y = _dot(a1_ref[...], w1_ref[...]) + _dot(a2_ref[...], w2_ref[...])
    o_ref[...] = x_ref[...] + _rms(y, gain_ref[...])


def _mix_out(a1, a2, w1, w2, x, gain):
    m, d = x.shape
    k1, k2 = a1.shape[1], a2.shape[1]
    tm = min(MIX_TM, m)
    return pl.pallas_call(
        _mix_out_kernel,
        grid=(m // tm,),
        in_specs=[
            pl.BlockSpec((tm, k1), lambda i: (i, 0)),
            pl.BlockSpec((tm, k2), lambda i: (i, 0)),
            pl.BlockSpec((k1, d), lambda i: (0, 0)),
            pl.BlockSpec((k2, d), lambda i: (0, 0)),
            pl.BlockSpec((tm, d), lambda i: (i, 0)),
            pl.BlockSpec((1, d), lambda i: (0, 0)),
        ],
        out_specs=pl.BlockSpec((tm, d), lambda i: (i, 0)),
        out_shape=jax.ShapeDtypeStruct((m, d), F32),
        compiler_params=_params("parallel"),
        name="mix_out",
    )(a1, a2, w1, w2, x, gain)


def _ffn_kernel(x_ref, gpre_ref, gpost_ref, up_ref, down_ref, o_ref, h_ref, acc_ref):
    f = pl.program_id(1)

    @pl.when(f == 0)
    def _():
        h_ref[...] = _rms(x_ref[...], gpre_ref[...]).astype(BF16)
        acc_ref[...] = jnp.zeros_like(acc_ref)

    a = jnp.maximum(jnp.dot(h_ref[...], up_ref[...], preferred_element_type=F32), 0.0)
    acc_ref[...] += jnp.dot((a * a).astype(BF16), down_ref[...], preferred_element_type=F32)

    @pl.when(f == pl.num_programs(1) - 1)
    def _():
        o_ref[...] = x_ref[...] + _rms(acc_ref[...], gpost_ref[...])


def _ffn(x, gpre, gpost, up, down):
    m, d = x.shape
    dff = up.shape[1]
    tm = min(FFN_TM, m)
    tf = min(FFN_TF, dff)
    return pl.pallas_call(
        _ffn_kernel,
        grid=(m // tm, dff // tf),
        in_specs=[
            pl.BlockSpec((tm, d), lambda i, f: (i, 0)),
            pl.BlockSpec((1, d), lambda i, f: (0, 0)),
            pl.BlockSpec((1, d), lambda i, f: (0, 0)),
            pl.BlockSpec((d, tf), lambda i, f: (0, f)),
            pl.BlockSpec((tf, d), lambda i, f: (f, 0)),
        ],
        out_specs=pl.BlockSpec((tm, d), lambda i, f: (i, 0)),
        out_shape=jax.ShapeDtypeStruct((m, d), F32),
        scratch_shapes=[pltpu.VMEM((tm, d), BF16), pltpu.VMEM((tm, d), F32)],
        compiler_params=_params("parallel", "arbitrary"),
        name="ffn",
    )(x, gpre, gpost, up, down)


def _ple_kernel(x_ref, gain_ref, wg_ref, p_ref, wp_ref, o_ref):
    x = x_ref[...]
    gate = _sigmoid(_dot(_rms(x, gain_ref[...]), wg_ref[...]))
    o_ref[...] = x + _dot(p_ref[...], wp_ref[...]) * gate


def _ple(x, gain, wg, p, wp):
    m, d = x.shape
    pd = p.shape[1]
    tm = min(PLE_TM, m)
    return pl.pallas_call(
        _ple_kernel,
        grid=(m // tm,),
        in_specs=[
            pl.BlockSpec((tm, d), lambda i: (i, 0)),
            pl.BlockSpec((1, d), lambda i: (0, 0)),
            pl.BlockSpec((d, d), lambda i: (0, 0)),
            pl.BlockSpec((tm, pd), lambda i: (i, 0)),
            pl.BlockSpec((pd, d), lambda i: (0, 0)),
        ],
        out_specs=pl.BlockSpec((tm, d), lambda i: (i, 0)),
        out_shape=jax.ShapeDtypeStruct((m, d), F32),
        compiler_params=_params("parallel"),
        name="ple",
    )(x, gain, wg, p, wp)


def _pool_kernel(u_ref, prev_ref, w_ref, scale_ref, o_ref, ext_ref):
    i = pl.program_id(1)
    ts = u_ref.shape[0]
    ext_ref[pl.ds(POOL_HALO, ts), :] = u_ref[...]

    @pl.when(i == 0)
    def _():
        ext_ref[pl.ds(0, POOL_HALO), :] = jnp.zeros((POOL_HALO, POOL_WIDTH), F32)

    @pl.when(i > 0)
    def _():
        ext_ref[pl.ds(0, POOL_HALO), :] = prev_ref[...]

    t = i * ts + lax.broadcasted_iota(jnp.int32, (ts, 1), 0)
    for gi, w in enumerate(POOL_WINDOWS):
        lo = gi * POOL_GROUP
        cur = ext_ref[pl.ds(POOL_HALO, ts), pl.ds(lo, POOL_GROUP)]
        tot = cur
        for j in range(1, w):
            tot = tot + ext_ref[pl.ds(POOL_HALO - j, ts), pl.ds(lo, POOL_GROUP)]
        cnt = jnp.minimum(t + 1, w).astype(F32)
        pooled = tot / cnt - cur
        o_ref[:, pl.ds(lo, POOL_GROUP)] = _dot(pooled, w_ref[gi]) * scale_ref[:, pl.ds(lo, POOL_GROUP)]


def _pool(u, pool_w, pool_scale, batch):
    m, width = u.shape
    s = m // batch
    ts = min(POOL_TS, s)
    nt = s // ts
    hb = ts // POOL_HALO
    return pl.pallas_call(
        _pool_kernel,
        grid=(batch, nt),
        in_specs=[
            pl.BlockSpec((ts, width), lambda b, i: (b * nt + i, 0)),
            pl.BlockSpec((POOL_HALO, width), lambda b, i: (jnp.maximum((b * nt + i) * hb - 1, 0), 0)),
            pl.BlockSpec((len(POOL_WINDOWS), POOL_GROUP, POOL_GROUP), lambda b, i: (0, 0, 0)),
            pl.BlockSpec((1, width), lambda b, i: (0, 0)),
        ],
        out_specs=pl.BlockSpec((ts, width), lambda b, i: (b * nt + i, 0)),
        out_shape=jax.ShapeDtypeStruct((m, width), F32),
        scratch_shapes=[pltpu.VMEM((POOL_HALO + ts, width), F32)],
        compiler_params=_params("parallel", "arbitrary"),
        name="pool_mixer",
    )(u, u, pool_w, pool_scale)


def _gla_constants(c):
    nlev = int(math.log2(c))
    i = np.arange(c)[:, None]
    j = np.arange(c)[None, :]
    mats = [j <= i, j > i]
    for lev in range(nlev):
        s = 1 << lev
        blk = (i // (2 * s)) * (2 * s)
        second = ((i // s) % 2) == 1
        mq = second & (j >= blk + s) & (j <= i)
        mk = (~second) & (j > i) & (j <= blk + s - 1)
        mats.append(mq | mk)
    sel = np.concatenate(mats, axis=0).astype(np.float32)
    x = i ^ j
    level = np.full((c, c), -1, np.int32)
    for lev in range(nlev):
        level = np.where((x >> lev) == 1, lev, level)
    level = np.where(j > i, -1, level)
    level = np.where(i == j, nlev, level)
    return jnp.asarray(sel, BF16), jnp.asarray(level, jnp.int32)


def _gla_kernel(q_ref, k_ref, v_ref, go_ref, glr_ref, w2_ref, gb_ref, gn_ref, sel_ref, lev_ref,
                o_ref, st_ref):
    c = GLA_CHUNK
    nlev = int(math.log2(c))
    ts = q_ref.shape[0]

    @pl.when(pl.program_id(2) == 0)
    def _():
        st_ref[...] = jnp.zeros_like(st_ref)

    sel = sel_ref[...]
    level = lev_ref[...]
    w2 = w2_ref[...]
    gb = gb_ref[...]
    gn = gn_ref[...]

    g_all = _log_sigmoid(_dot3(glr_ref[...], w2) + gb) / GLA_TAU
    st = st_ref[...]
    for gi in range(ts // (c * GLA_GROUP)):
        ids = range(gi * GLA_GROUP, (gi + 1) * GLA_GROUP)
        q = {ci: q_ref[pl.ds(ci * c, c), :] * (GLA_DK ** -0.5) for ci in ids}
        k = {ci: k_ref[pl.ds(ci * c, c), :] for ci in ids}
        e = {ci: jnp.exp(_dot_sel(sel, g_all[ci * c:(ci + 1) * c])) for ci in ids}
        att = {ci: jnp.where(level == nlev, _dot_nt(q[ci], k[ci]), 0.0) for ci in ids}
        for lev in range(nlev):
            lo, hi = (2 + lev) * c, (3 + lev) * c
            att = {ci: jnp.where(level == lev, _dot_nt(q[ci] * e[ci][lo:hi], k[ci] * e[ci][lo:hi]), att[ci])
                   for ci in ids}
        intra = {ci: _dot(att[ci], v_ref[pl.ds(ci * c, c), :]) for ci in ids}
        qe = {ci: q[ci] * e[ci][0:c] for ci in ids}
        ke = {ci: k[ci] * e[ci][c:2 * c] for ci in ids}
        for ci in ids:
            o = _dot_nt(qe[ci], st) + intra[ci]
            st = st * e[ci][c - 1:c] + _dot_tn(v_ref[pl.ds(ci * c, c), :], ke[ci])
            o = o * lax.rsqrt(jnp.mean(o * o, axis=-1, keepdims=True) + NORM_EPS) * gn
            gate = go_ref[pl.ds(ci * c, c), :]
            o_ref[pl.ds(ci * c, c), :] = o * (gate * _sigmoid(gate))
    st_ref[...] = st


def _gla(qk, vg, glr, w2p, gate_b, gla_norm, batch):
    m = qk.shape[1]
    s = m // batch
    ts = min(GLA_TS, s)
    nt = s // ts
    h = GLA_HEADS
    sel, level = _gla_constants(GLA_CHUNK)
    row = lambda b, hh, i: b * nt + i
    return pl.pallas_call(
        _gla_kernel,
        grid=(batch, h, nt),
        in_specs=[
            pl.BlockSpec((None, ts, GLA_DK), lambda b, hh, i: (hh, row(b, hh, i), 0)),
            pl.BlockSpec((None, ts, GLA_DK), lambda b, hh, i: (h + hh, row(b, hh, i), 0)),
            pl.BlockSpec((None, ts, GLA_DV), lambda b, hh, i: (hh, row(b, hh, i), 0)),
            pl.BlockSpec((None, ts, GLA_DV), lambda b, hh, i: (h + hh, row(b, hh, i), 0)),
            pl.BlockSpec((ts, LANES), lambda b, hh, i: (row(b, hh, i), 0)),
            pl.BlockSpec((None, LANES, GLA_DK), lambda b, hh, i: (hh, 0, 0)),
            pl.BlockSpec((None, 1, GLA_DK), lambda b, hh, i: (hh, 0, 0)),
            pl.BlockSpec((1, GLA_DV), lambda b, hh, i: (0, 0)),
            pl.BlockSpec(sel.shape, lambda b, hh, i: (0, 0)),
            pl.BlockSpec(level.shape, lambda b, hh, i: (0, 0)),
        ],
        out_specs=pl.BlockSpec((ts, GLA_DV), lambda b, hh, i: (row(b, hh, i), hh)),
        out_shape=jax.ShapeDtypeStruct((m, GLA_WIDTH), F32),
        scratch_shapes=[pltpu.VMEM((GLA_DV, GLA_DK), F32)],
        compiler_params=_params("parallel", "parallel", "arbitrary"),
        name="gla",
    )(qk, qk, vg, vg, glr, w2p, gate_b, gla_norm, sel, level)


def _head_sum_matrix():
    lane = np.arange(LANES)
    return jnp.asarray((lane[:, None] // RWKV_HEAD) == (lane[None, :] // RWKV_HEAD), BF16)


def _shifted(cur, prev8, first):
    prev_row = jnp.where(first, 0.0, prev8[7:8, :])
    rolled = pltpu.roll(cur, 1, 0)
    row = lax.broadcasted_iota(jnp.int32, cur.shape, 0)
    return jnp.where(row == 0, prev_row, rolled)


def _rwkv_prepare(first, rkv_ref, rkvp_ref, lr_ref, lrp_ref, mu_ref, mulr_ref, w0_ref, w2_ref, a0_ref,
                  a2_ref, g2_ref, kk_ref, ka_ref, hs_ref,
                  r_out, lw_out, k_out, v_out, a_out, b_out, g_out):
    def mixed(cur, prev8, mu):
        return cur + (_shifted(cur, prev8, first) - cur) * mu

    r = mixed(rkv_ref[0], rkvp_ref[0], mu_ref[0])
    k = mixed(rkv_ref[1], rkvp_ref[1], mu_ref[1])
    v = mixed(rkv_ref[2], rkvp_ref[2], mu_ref[2])
    lr = mixed(lr_ref[...], lrp_ref[...], mulr_ref[...])

    lr_wa = lr[:, 0:LANES]
    lr_g = lr[:, LANES:RWKV_LR_PAD]
    w_log = _log_sigmoid(w0_ref[...] + _dot(jnp.tanh(lr_wa), w2_ref[...])) - 0.5
    lw = -jnp.exp(w_log)
    a = _sigmoid(a0_ref[...] + _dot(lr_wa, a2_ref[...]))
    g = _dot(_sigmoid(lr_g), g2_ref[...])

    kk = k * kk_ref[...]
    norm = jnp.sqrt(_dot(kk * kk, hs_ref[...]))
    kk = kk / jnp.maximum(norm, 1e-12)
    r_out[...] = r
    lw_out[...] = lw
    k_out[...] = k * (1.0 + (a - 1.0) * ka_ref[...])
    v_out[...] = v
    a_out[...] = -kk
    b_out[...] = kk * a
    g_out[...] = g


def _rwkv_constants(c):
    i = np.arange(c)[:, None]
    j = np.arange(c)[None, :]
    tri = jnp.asarray(j <= i, BF16)
    i2 = np.arange(2 * c)[:, None] % c
    j2 = np.arange(2 * c)[None, :] % c
    strict = jnp.asarray(i2 > j2, F32)
    incl = jnp.asarray(i2 >= j2, F32)
    return tri, strict, incl


def _rwkv_kernel(rkv_ref, rkvp_ref, lr_ref, lrp_ref, mu_ref, mulr_ref, w0_ref, w2_ref, a0_ref, a2_ref, g2_ref,
                 kk_ref, ka_ref, lnw_ref, lnb_ref, rk_ref, tri_ref, strict_ref, incl_ref, hs_ref, o_ref,
                 state_ref, p_ref, gm_ref, rq_ref, y0_ref, r_ref, lw_ref, k_ref, v_ref, a_ref, b_ref, g_ref):
    c = RWKV_CHUNK
    nlev = int(math.log2(c))
    ts = o_ref.shape[0]
    nchunk = ts // c
    first = pl.program_id(2) == 0

    @pl.when(first)
    def _():
        state_ref[...] = jnp.zeros_like(state_ref)

    _rwkv_prepare(first, rkv_ref, rkvp_ref, lr_ref, lrp_ref, mu_ref, mulr_ref, w0_ref, w2_ref, a0_ref, a2_ref,
                  g2_ref, kk_ref, ka_ref, hs_ref, r_ref, lw_ref, k_ref, v_ref, a_ref, b_ref, g_ref)

    lane = lax.broadcasted_iota(jnp.int32, (1, LANES), 1)
    head0 = lane < RWKV_HEAD
    tri = tri_ref[...]
    strict = strict_ref[...]
    incl = incl_ref[...]
    eye = (lax.broadcasted_iota(jnp.int32, (LANES, LANES), 0)
           == lax.broadcasted_iota(jnp.int32, (LANES, LANES), 1)).astype(F32)

    def stack(x):
        return jnp.concatenate([jnp.where(head0, x, 0.0), jnp.where(head0, 0.0, x)], axis=0)

    grp = RWKV_GROUP
    each = range(grp)

    def local(gi):
        base = gi * grp
        rows = [pl.ds((base + t) * c, c) for t in each]
        lw = [lw_ref[rows[t], :] for t in each]
        cum = [_dot_sel(tri, lw[t]) for t in each]
        last = [cum[t][c - 1:c] for t in each]
        grow = [jnp.exp(-cum[t]) for t in each]
        fall = [jnp.exp(last[t] - cum[t]) for t in each]
        at2 = [stack(a_ref[rows[t], :] * jnp.exp(cum[t] - lw[t])) for t in each]
        rt2 = [stack(r_ref[rows[t], :] * jnp.exp(cum[t])) for t in each]
        bt2 = [stack(b_ref[rows[t], :] * grow[t]) for t in each]
        kt2 = [stack(k_ref[rows[t], :] * grow[t]) for t in each]
        bh2 = [stack(b_ref[rows[t], :] * fall[t]) for t in each]
        kh2 = [stack(k_ref[rows[t], :] * fall[t]) for t in each]
        v2 = [stack(v_ref[rows[t], :]) for t in each]
        big = [_dot_nt(jnp.concatenate([at2[t], rt2[t]], axis=0), jnp.concatenate([bt2[t], kt2[t]], axis=0))
               for t in each]
        a_ab = [big[t][0:2 * c, 0:2 * c] * strict for t in each]
        a_ak = [big[t][0:2 * c, 2 * c:4 * c] * strict for t in each]
        a_rb = [big[t][2 * c:4 * c, 0:2 * c] * incl for t in each]
        a_rk = [big[t][2 * c:4 * c, 2 * c:4 * c] * incl for t in each]
        rhs = [jnp.concatenate([at2[t], _dot(a_ak[t], v2[t])], axis=1) for t in each]
        n = a_ab
        for lev in range(nlev):
            rhs = [rhs[t] + _dot(n[t], rhs[t]) for t in each]
            if lev + 1 < nlev:
                n = [_dot(n[t], n[t]) for t in each]
        for t in each:
            w2 = rhs[t][:, 0:LANES]
            z2 = rhs[t][:, LANES:2 * LANES]
            gamma = jnp.exp(last[t])
            p_ref[base + t] = eye * gamma + _dot_tn(w2, bh2[t])
            gm_ref[base + t] = _dot_tn(jnp.concatenate([z2, v2[t]], axis=0),
                                       jnp.concatenate([bh2[t], kh2[t]], axis=0))
            rq_ref[base + t] = rt2[t] + _dot(a_rb[t], w2)
            y0_ref[base + t] = _dot(a_rb[t], z2) + _dot(a_rk[t], v2[t])

    st = state_ref[...]
    for gi in range(nchunk // grp):
        local(gi)
        for ci in range(gi * grp, (gi + 1) * grp):
            y2 = _dot3_nt(rq_ref[ci], st) + y0_ref[ci]
            o_ref[pl.ds(ci * c, c), :] = y2[0:c] + y2[c:2 * c]
            st = _dot3(st, p_ref[ci]) + gm_ref[ci]
    state_ref[...] = st

    hs = hs_ref[...]
    y = o_ref[...]
    mean = _dot(y, hs) * (1.0 / RWKV_HEAD)
    yc = y - mean
    var = _dot(yc * yc, hs) * (1.0 / RWKV_HEAD)
    yn = yc * lax.rsqrt(var + RWKV_LN_EPS) * lnw_ref[...] + lnb_ref[...]
    bonus = _dot(r_ref[...] * k_ref[...] * rk_ref[...], hs) * v_ref[...]
    o_ref[...] = (yn + bonus) * g_ref[...]


def _rwkv(z6, lr, mu_rkv, mu_lr, w0, w2p, a0, a2p, g2p, k_k, k_a, ln_w, ln_b, r_k, batch):
    m = lr.shape[0]
    s = m // batch
    ts = min(RWKV_TS, s)
    nt = s // ts
    nb = RWKV_WIDTH // LANES
    c = RWKV_CHUNK
    nchunk = ts // c
    tri, strict, incl = _rwkv_constants(c)
    hs = _head_sum_matrix()
    row = lambda bb, j, i: bb * nt + i
    prow = lambda bb, j, i: jnp.maximum((bb * nt + i) * (ts // 8) - 1, 0)
    vecspec = pl.BlockSpec((1, LANES), lambda bb, j, i: (0, j))
    lrw = pl.BlockSpec((LANES, LANES), lambda bb, j, i: (0, j))
    lrg = pl.BlockSpec((RWKV_LR_PAD - LANES, LANES), lambda bb, j, i: (0, j))
    const = lambda arr: pl.BlockSpec(arr.shape, lambda bb, j, i: (0,) * arr.ndim)
    tile = pltpu.VMEM((ts, LANES), F32)
    return pl.pallas_call(
        _rwkv_kernel,
        grid=(batch, nb, nt),
        in_specs=[
            pl.BlockSpec((3, ts, LANES), lambda bb, j, i: (0, row(bb, j, i), j)),
            pl.BlockSpec((3, 8, LANES), lambda bb, j, i: (0, prow(bb, j, i), j)),
            pl.BlockSpec((ts, RWKV_LR_PAD), lambda bb, j, i: (row(bb, j, i), 0)),
            pl.BlockSpec((8, RWKV_LR_PAD), lambda bb, j, i: (prow(bb, j, i), 0)),
            pl.BlockSpec((3, 1, LANES), lambda bb, j, i: (0, 0, j)),
            pl.BlockSpec((1, RWKV_LR_PAD), lambda bb, j, i: (0, 0)),
            vecspec, lrw, vecspec, lrw, lrg, vecspec, vecspec,
            vecspec, vecspec, vecspec,
            const(tri), const(strict), const(incl), const(hs),
        ],
        out_specs=pl.BlockSpec((ts, LANES), lambda bb, j, i: (row(bb, j, i), j)),
        out_shape=jax.ShapeDtypeStruct((m, RWKV_WIDTH), F32),
        scratch_shapes=[
            pltpu.VMEM((LANES, LANES), F32),
            pltpu.VMEM((nchunk, LANES, LANES), F32),
            pltpu.VMEM((nchunk, LANES, LANES), F32),
            pltpu.VMEM((nchunk, 2 * c, LANES), F32),
            pltpu.VMEM((nchunk, 2 * c, LANES), F32),
        ] + [tile] * 7,
        compiler_params=_params("parallel", "parallel", "arbitrary"),
        name="rwkv",
    )(z6, z6, lr, lr, mu_rkv, mu_lr, w0, w2p, a0, a2p, g2p, k_k, k_a, ln_w, ln_b, r_k, tri, strict, incl, hs)


def _rope(x, cos2, sin2):
    return x * cos2 + pltpu.roll(x, DIL_HEAD // 2, 1) * sin2


def _dil_kernel(q_ref, kp_ref, kc_ref, vp_ref, vc_ref, cos_ref, sin_ref, cosp_ref, sinp_ref, o_ref,
                qs_ref, ks_ref, vs_ref, acc_ref, m_ref, l_ref):
    ts = q_ref.shape[0]
    i = pl.program_id(2)
    blk = DIL_BLOCK
    qs_ref[...] = _rope(q_ref[...], cos_ref[...], sin_ref[...]) * (DIL_HEAD ** -0.5)
    ks_ref[pl.ds(0, ts), :] = _rope(kp_ref[...], cosp_ref[...], sinp_ref[...])
    ks_ref[pl.ds(ts, ts), :] = _rope(kc_ref[...], cos_ref[...], sin_ref[...])
    vs_ref[pl.ds(0, ts), :] = vp_ref[...]
    vs_ref[pl.ds(ts, ts), :] = vc_ref[...]

    qi = lax.broadcasted_iota(jnp.int32, (blk, 2 * blk), 0)
    ki = lax.broadcasted_iota(jnp.int32, (blk, 2 * blk), 1)
    dist = qi + blk - ki
    band = (dist >= 0) & (dist <= blk)

    for pi, (window, dil) in enumerate(DIL_PATTERNS):
        assert window // dil == blk
        per_res = ts // (blk * dil)

        def unit(u, carry, pi=pi, dil=dil, per_res=per_res):
            res = u // per_res
            nblk = u % per_res
            qstart = res + nblk * (blk * dil)
            kstart = ts + qstart - blk * dil
            q = qs_ref[pl.ds(qstart, blk, stride=dil), :]
            k = ks_ref[pl.ds(kstart, 2 * blk, stride=dil), :]
            v = vs_ref[pl.ds(kstart, 2 * blk, stride=dil), :]
            s = _dot_nt(q, k)
            key_pos = i * ts + (qstart - blk * dil) + ki * dil
            s = jnp.where(band & (key_pos >= 0), s, -jnp.inf)
            mx = jnp.max(s, axis=-1, keepdims=True)
            p = jnp.exp(s - mx)
            acc_ref[pi, pl.ds(qstart, blk, stride=dil), :] = _dot(p, v)
            m_ref[pi, pl.ds(qstart, blk, stride=dil), :] = jnp.broadcast_to(mx, (blk, LANES))
            l_ref[pi, pl.ds(qstart, blk, stride=dil), :] = jnp.broadcast_to(
                jnp.sum(p, axis=-1, keepdims=True), (blk, LANES))
            return carry

        lax.fori_loop(0, ts // blk, unit, 0, unroll=8)

    m_all = jnp.maximum(jnp.maximum(m_ref[0], m_ref[1]), m_ref[2])
    num = jnp.zeros((ts, LANES), F32)
    den = jnp.zeros((ts, LANES), F32)
    for pi in range(len(DIL_PATTERNS)):
        cf = jnp.exp(m_ref[pi] - m_all)
        num = num + cf * acc_ref[pi]
        den = den + cf * l_ref[pi]
    o_ref[...] = num / den


def _dilated(z6, cos2, sin2, batch):
    m = z6.shape[1]
    s = m // batch
    ts = DIL_TS
    assert s % ts == 0
    nt = s // ts
    cur = lambda which: pl.BlockSpec((None, ts, DIL_HEAD), lambda b, h, i: (which, b * nt + i, h))
    prev = lambda which: pl.BlockSpec((None, ts, DIL_HEAD),
                                      lambda b, h, i: (which, b * nt + jnp.maximum(i - 1, 0), h))
    tab = pl.BlockSpec((ts, DIL_HEAD), lambda b, h, i: (i, 0))
    tabp = pl.BlockSpec((ts, DIL_HEAD), lambda b, h, i: (jnp.maximum(i - 1, 0), 0))
    npat = len(DIL_PATTERNS)
    return pl.pallas_call(
        _dil_kernel,
        grid=(batch, DIL_HEADS, nt),
        in_specs=[cur(3), prev(4), cur(4), prev(5), cur(5), tab, tab, tabp, tabp],
        out_specs=pl.BlockSpec((ts, DIL_HEAD), lambda b, h, i: (b * nt + i, h)),
        out_shape=jax.ShapeDtypeStruct((m, DIL_WIDTH), F32),
        scratch_shapes=[
            pltpu.VMEM((ts, DIL_HEAD), F32),
            pltpu.VMEM((2 * ts, DIL_HEAD), F32),
            pltpu.VMEM((2 * ts, DIL_HEAD), F32),
            pltpu.VMEM((npat, ts, DIL_HEAD), F32),
            pltpu.VMEM((npat, ts, LANES), F32),
            pltpu.VMEM((npat, ts, LANES), F32),
        ],
        compiler_params=_params("parallel", "parallel", "arbitrary"),
        name="dilated_attention",
    )(z6, z6, z6, z6, z6, cos2, sin2, cos2, sin2)


def _heads(w, n_heads, width):
    d = w.shape[0]
    return w.reshape(d, n_heads, width).transpose(1, 0, 2)


def _pad_rows(w, start, total):
    return jnp.zeros((total, w.shape[1]), w.dtype).at[start:start + w.shape[0]].set(w)


def _even_mixer(x, gain_pre, gain_post, w_in, w_out, pool_w, pool_scale, gate_w2, gate_b, gla_norm, batch):
    d = x.shape[1]
    o0 = POOL_WIDTH
    o1 = o0 + GLA_QK
    o2 = o1 + GLA_QK
    o3 = o2 + GLA_WIDTH
    o4 = o3 + GLA_WIDTH
    wb = w_in.astype(BF16)
    w_u = wb[:, :o0][None]
    w_qk = jnp.concatenate([_heads(wb[:, o0:o1], GLA_HEADS, GLA_DK), _heads(wb[:, o1:o2], GLA_HEADS, GLA_DK)], 0)
    w_vg = jnp.concatenate([_heads(wb[:, o2:o3], GLA_HEADS, GLA_DV), _heads(wb[:, o3:o4], GLA_HEADS, GLA_DV)], 0)
    w_glr = jnp.zeros((d, LANES), BF16).at[:, :GLA_GATE_RANK].set(wb[:, o4:])[None]
    u, qk, vg, glr = _proj(x, gain_pre, [w_u, w_qk, w_vg, w_glr])
    u, glr = u[0], glr[0]
    a_out = _pool(u, pool_w.astype(BF16), pool_scale[None], batch)
    w2p = _heads(_pad_rows(gate_w2, 0, LANES), GLA_HEADS, GLA_DK)
    o = _gla(qk, vg, glr, w2p, gate_b.reshape(GLA_HEADS, 1, GLA_DK), gla_norm[None], batch)
    wo = w_out.astype(BF16)
    return _mix_out(a_out, o, wo[:POOL_WIDTH], wo[POOL_WIDTH:], x, gain_post)


def _odd_mixer(x, gain_pre, gain_post, w_in, w_out, mu, w0, w2, a0, a2, g2, k_k, k_a, r_k, ln_w, ln_b,
               cos2, sin2, batch):
    d = x.shape[1]
    wb = w_in.astype(BF16)
    c3 = 3 * RWKV_WIDTH
    rwkv_in = c3 + RWKV_LR
    w6 = jnp.concatenate([_heads(wb[:, :c3], 3, RWKV_WIDTH), _heads(wb[:, rwkv_in:], 3, DIL_WIDTH)], 0)
    w_lr = jnp.zeros((d, RWKV_LR_PAD), BF16).at[:, :RWKV_LR].set(wb[:, c3:rwkv_in])[None]
    z6, lr = _proj(x, gain_pre, [w6, w_lr])
    lr = lr[0]
    mu_rkv = mu[:c3].reshape(3, 1, RWKV_WIDTH)
    mu_lr = jnp.zeros((1, RWKV_LR_PAD), F32).at[0, :RWKV_LR].set(mu[c3:])
    w2p = _pad_rows(w2, 0, LANES).astype(BF16)
    a2p = _pad_rows(a2, RWKV_DECAY_RANK, LANES).astype(BF16)
    g2p = _pad_rows(g2, 0, RWKV_LR_PAD - LANES).astype(BF16)
    c_out = _rwkv(z6, lr, mu_rkv, mu_lr, w0[None], w2p, a0[None], a2p, g2p, k_k[None], k_a[None],
                  ln_w[None], ln_b[None], r_k.reshape(1, RWKV_WIDTH), batch)
    d_out = _dilated(z6, cos2, sin2, batch)
    wo = w_out.astype(BF16)
    return _mix_out(c_out, d_out, wo[:RWKV_WIDTH], wo[RWKV_WIDTH:], x, gain_post)


def _rope_tables(s):
    half = DIL_HEAD // 2
    inv = ROPE_THETA ** (-jnp.arange(half, dtype=F32) / half)
    ang = jnp.arange(s).astype(F32)[:, None] * inv[None, :]
    cos, sin = jnp.cos(ang), jnp.sin(ang)
    return jnp.concatenate([cos, cos], axis=-1), jnp.concatenate([-sin, sin], axis=-1)


def kernel(x, p, norm_mix_pre, norm_mix_post, norm_ffn_pre, norm_ffn_post, ev_w_in, ev_w_out, pool_w, pool_scale, gla_gate_w2, gla_gate_b, gla_norm, od_w_in, od_w_out, rwkv_mu, rwkv_w0, rwkv_w2, rwkv_a0, rwkv_a2, rwkv_g2, rwkv_k_k, rwkv_k_a, rwkv_r_k, rwkv_ln_w, rwkv_ln_b, ffn_up, ffn_down, ple_proj, ple_gate, ple_norm):
    batch, s, d = x.shape
    depth = p.shape[0]
    m = batch * s
    xf = x.reshape(m, d)
    cos2, sin2 = _rope_tables(s)
    for i in range(depth):
        j = i // 2
        if i % 2 == 0:
            xf = _even_mixer(xf, norm_mix_pre[i][None], norm_mix_post[i][None], ev_w_in[j], ev_w_out[j],
                             pool_w[j], pool_scale[j], gla_gate_w2[j], gla_gate_b[j], gla_norm[j], batch)
        else:
            xf = _odd_mixer(xf, norm_mix_pre[i][None], norm_mix_post[i][None], od_w_in[j], od_w_out[j],
                            rwkv_mu[j], rwkv_w0[j], rwkv_w2[j], rwkv_a0[j], rwkv_a2[j], rwkv_g2[j],
                            rwkv_k_k[j], rwkv_k_a[j], rwkv_r_k[j], rwkv_ln_w[j], rwkv_ln_b[j],
                            cos2, sin2, batch)
        xf = _ffn(xf, norm_ffn_pre[i][None], norm_ffn_post[i][None],
                  ffn_up[i].astype(BF16), ffn_down[i].astype(BF16))
        xf = _ple(xf, ple_norm[i][None], ple_gate[i].astype(BF16), p[i].reshape(m, PLE_DIM),
                  ple_proj[i].astype(BF16))
    return xf.reshape(batch, s, d)
```

```python
import functools
import math

import numpy as np
import jax
import jax.numpy as jnp
from jax import lax
from jax.experimental import pallas as pl
from jax.experimental.pallas import tpu as pltpu

F32 = jnp.float32
BF16 = jnp.bfloat16

NORM_EPS = 1e-6
LANES = 128
VMEM_LIMIT_BYTES = 56 * 1024 * 1024

D_MODEL = 2048
PLE_DIM = 256
POOL_WINDOWS = (2, 4, 8, 16)
POOL_GROUP = 128
POOL_WIDTH = 512
POOL_HALO = 16
GLA_HEADS = 4
GLA_DK = 192
GLA_DV = 384
GLA_QK = GLA_HEADS * GLA_DK
GLA_WIDTH = GLA_HEADS * GLA_DV
GLA_GATE_RANK = 16
GLA_TAU = 16.0
RWKV_HEAD = 64
RWKV_WIDTH = 1024
RWKV_DECAY_RANK = 64
RWKV_A_RANK = 64
RWKV_GATE_RANK = 160
RWKV_LR = RWKV_DECAY_RANK + RWKV_A_RANK + RWKV_GATE_RANK
RWKV_LR_PAD = 384
RWKV_LN_EPS = 64e-5
DIL_WIDTH = 1024
DIL_HEAD = 128
DIL_HEADS = 8
DIL_PATTERNS = ((128, 1), (512, 4), (2048, 16))
DIL_BLOCK = 128
ROPE_THETA = 10000.0

PROJ_TM = 1024
MIX_TM = 512
FFN_TM = 512
FFN_TF = 1024
PLE_TM = 512
POOL_TS = 512
GLA_CHUNK = 64
GLA_TS = 1024
GLA_GROUP = 8
RWKV_CHUNK = 64
RWKV_TS = 1024
RWKV_GROUP = 8
DIL_TS = 2048


def _params(*sem):
    return pltpu.CompilerParams(dimension_semantics=sem, vmem_limit_bytes=VMEM_LIMIT_BYTES)


def _dot(a, b):
    return jnp.dot(a.astype(BF16), b.astype(BF16), preferred_element_type=F32)


def _dot_nt(a, b):
    return lax.dot_general(a.astype(BF16), b.astype(BF16), (((1,), (1,)), ((), ())),
                           preferred_element_type=F32)


def _dot_tn(a, b):
    return lax.dot_general(a.astype(BF16), b.astype(BF16), (((0,), (0,)), ((), ())),
                           preferred_element_type=F32)


def _split(x):
    hi = x.astype(BF16)
    return hi, (x - hi.astype(F32)).astype(BF16)


def _dot3_general(a, b, dims):
    ah, al = _split(a)
    bh, bl = _split(b)
    d = lambda u, v: lax.dot_general(u, v, (dims, ((), ())), preferred_element_type=F32)
    return d(ah, bh) + d(ah, bl) + d(al, bh)


def _dot3(a, b):
    return _dot3_general(a, b, ((1,), (0,)))


def _dot3_nt(a, b):
    return _dot3_general(a, b, ((1,), (1,)))


def _dot_sel(sel_bf16, x):
    hi = x.astype(BF16)
    r1 = x - hi.astype(F32)
    mid = r1.astype(BF16)
    lo = (r1 - mid.astype(F32)).astype(BF16)
    out = jnp.dot(sel_bf16, hi, preferred_element_type=F32)
    out = out + jnp.dot(sel_bf16, mid, preferred_element_type=F32)
    return out + jnp.dot(sel_bf16, lo, preferred_element_type=F32)


def _rms(x, gain):
    return x * lax.rsqrt(jnp.mean(x * x, axis=-1, keepdims=True) + NORM_EPS) * gain


def _sigmoid(x):
    return 1.0 / (1.0 + jnp.exp(-x))


def _log_sigmoid(x):
    return jnp.minimum(x, 0.0) - jnp.log1p(jnp.exp(-jnp.abs(x)))


def _proj_kernel(starts, x_ref, gain_ref, *refs):
    nw = len(starts) - 1
    w_refs, o_refs, h_ref = refs[:nw], refs[nw:2 * nw], refs[2 * nw]
    j = pl.program_id(1)

    @pl.when(j == 0)
    def _():
        h_ref[...] = _rms(x_ref[...], gain_ref[...]).astype(BF16)

    for k in range(nw):
        @pl.when((j >= starts[k]) & (j < starts[k + 1]))
        def _(k=k):
            o_refs[k][...] = jnp.dot(h_ref[...], w_refs[k][...], preferred_element_type=F32)


def _proj(x, gain, ws):
    m, d = x.shape
    tm = min(PROJ_TM, m)
    starts = [0]
    for w in ws:
        starts.append(starts[-1] + w.shape[0])

    def group(k):
        return lambda i, j: jnp.clip(j - starts[k], 0, ws[k].shape[0] - 1)

    w_specs = [pl.BlockSpec((None, d, w.shape[2]), lambda i, j, gk=group(k): (gk(i, j), 0, 0))
               for k, w in enumerate(ws)]
    o_specs = [pl.BlockSpec((None, tm, w.shape[2]), lambda i, j, gk=group(k): (gk(i, j), i, 0))
               for k, w in enumerate(ws)]
    return pl.pallas_call(
        functools.partial(_proj_kernel, tuple(starts)),
        grid=(m // tm, starts[-1]),
        in_specs=[pl.BlockSpec((tm, d), lambda i, j: (i, 0)), pl.BlockSpec((1, d), lambda i, j: (0, 0))] + w_specs,
        out_specs=o_specs,
        out_shape=[jax.ShapeDtypeStruct((w.shape[0], m, w.shape[2]), F32) for w in ws],
        scratch_shapes=[pltpu.VMEM((tm, d), BF16)],
        compiler_params=_params("parallel", "arbitrary"),
        name="norm_proj",
    )(x, gain, *ws)


def _mix_out_kernel(a1_ref, a2_ref, w1_ref, w2_ref, x_ref, gain_ref, o_ref):
    y = _dot(a1_ref[...], w1_ref[...]) + _dot(a2_ref[...], w2_ref[...])
    o_ref[...] = x_ref[...] + _rms(y, gain_ref[...])


def _mix_out(a1, a2, w1, w2, x, gain):
    m, d = x.shape
    k1, k2 = a1.shape[1], a2.shape[1]
    tm = min(MIX_TM, m)
    return pl.pallas_call(
        _mix_out_kernel,
        grid=(m // tm,),
        in_specs=[
            pl.BlockSpec((tm, k1), lambda i: (i, 0)),
            pl.BlockSpec((tm, k2), lambda i: (i, 0)),
            pl.BlockSpec((k1, d), lambda i: (0, 0)),
            pl.BlockSpec((k2, d), lambda i: (0, 0)),
            pl.BlockSpec((tm, d), lambda i: (i, 0)),
            pl.BlockSpec((1, d), lambda i: (0, 0)),
        ],
        out_specs=pl.BlockSpec((tm, d), lambda i: (i, 0)),
        out_shape=jax.ShapeDtypeStruct((m, d), F32),
        compiler_params=_params("parallel"),
        name="mix_out",
    )(a1, a2, w1, w2, x, gain)


def _ffn_kernel(x_ref, gpre_ref, gpost_ref, up_ref, down_ref, o_ref, h_ref, acc_ref):
    f = pl.program_id(1)

    @pl.when(f == 0)
    def _():
        h_ref[...] = _rms(x_ref[...], gpre_ref[...]).astype(BF16)
        acc_ref[...] = jnp.zeros_like(acc_ref)

    a = jnp.maximum(jnp.dot(h_ref[...], up_ref[...], preferred_element_type=F32), 0.0)
    acc_ref[...] += jnp.dot((a * a).astype(BF16), down_ref[...], preferred_element_type=F32)

    @pl.when(f == pl.num_programs(1) - 1)
    def _():
        o_ref[...] = x_ref[...] + _rms(acc_ref[...], gpost_ref[...])


def _ffn(x, gpre, gpost, up, down):
    m, d = x.shape
    dff = up.shape[1]
    tm = min(FFN_TM, m)
    tf = min(FFN_TF, dff)
    return pl.pallas_call(
        _ffn_kernel,
        grid=(m // tm, dff // tf),
        in_specs=[
            pl.BlockSpec((tm, d), lambda i, f: (i, 0)),
            pl.BlockSpec((1, d), lambda i, f: (0, 0)),
            pl.BlockSpec((1, d), lambda i, f: (0, 0)),
            pl.BlockSpec((d, tf), lambda i, f: (0, f)),
            pl.BlockSpec((tf, d), lambda i, f: (f, 0)),
        ],
        out_specs=pl.BlockSpec((tm, d), lambda i, f: (i, 0)),
        out_shape=jax.ShapeDtypeStruct((m, d), F32),
        scratch_shapes=[pltpu.VMEM((tm, d), BF16), pltpu.VMEM((tm, d), F32)],
        compiler_params=_params("parallel", "arbitrary"),
        name="ffn",
    )(x, gpre, gpost, up, down)


def _ple_kernel(x_ref, gain_ref, wg_ref, p_ref, wp_ref, o_ref):
    x = x_ref[...]
    gate = _sigmoid(_dot(_rms(x, gain_ref[...]), wg_ref[...]))
    o_ref[...] = x + _dot(p_ref[...], wp_ref[...]) * gate


def _ple(x, gain, wg, p, wp):
    m, d = x.shape
    pd = p.shape[1]
    tm = min(PLE_TM, m)
    return pl.pallas_call(
        _ple_kernel,
        grid=(m // tm,),
        in_specs=[
            pl.BlockSpec((tm, d), lambda i: (i, 0)),
            pl.BlockSpec((1, d), lambda i: (0, 0)),
            pl.BlockSpec((d, d), lambda i: (0, 0)),
            pl.BlockSpec((tm, pd), lambda i: (i, 0)),
            pl.BlockSpec((pd, d), lambda i: (0, 0)),
        ],
        out_specs=pl.BlockSpec((tm, d), lambda i: (i, 0)),
        out_shape=jax.ShapeDtypeStruct((m, d), F32),
        compiler_params=_params("parallel"),
        name="ple",
    )(x, gain, wg, p, wp)


def _pool_kernel(u_ref, prev_ref, w_ref, scale_ref, o_ref, ext_ref):
    i = pl.program_id(1)
    ts = u_ref.shape[0]
    ext_ref[pl.ds(POOL_HALO, ts), :] = u_ref[...]

    @pl.when(i == 0)
    def _():
        ext_ref[pl.ds(0, POOL_HALO), :] = jnp.zeros((POOL_HALO, POOL_WIDTH), F32)

    @pl.when(i > 0)
    def _():
        ext_ref[pl.ds(0, POOL_HALO), :] = prev_ref[...]

    t = i * ts + lax.broadcasted_iota(jnp.int32, (ts, 1), 0)
    for gi, w in enumerate(POOL_WINDOWS):
        lo = gi * POOL_GROUP
        cur = ext_ref[pl.ds(POOL_HALO, ts), pl.ds(lo, POOL_GROUP)]
        tot = cur
        for j in range(1, w):
            tot = tot + ext_ref[pl.ds(POOL_HALO - j, ts), pl.ds(lo, POOL_GROUP)]
        cnt = jnp.minimum(t + 1, w).astype(F32)
        pooled = tot / cnt - cur
        o_ref[:, pl.ds(lo, POOL_GROUP)] = _dot(pooled, w_ref[gi]) * scale_ref[:, pl.ds(lo, POOL_GROUP)]


def _pool(u, pool_w, pool_scale, batch):
    m, width = u.shape
    s = m // batch
    ts = min(POOL_TS, s)
    nt = s // ts
    hb = ts // POOL_HALO
    return pl.pallas_call(
        _pool_kernel,
        grid=(batch, nt),
        in_specs=[
            pl.BlockSpec((ts, width), lambda b, i: (b * nt + i, 0)),
            pl.BlockSpec((POOL_HALO, width), lambda b, i: (jnp.maximum((b * nt + i) * hb - 1, 0), 0)),
            pl.BlockSpec((len(POOL_WINDOWS), POOL_GROUP, POOL_GROUP), lambda b, i: (0, 0, 0)),
            pl.BlockSpec((1, width), lambda b, i: (0, 0)),
        ],
        out_specs=pl.BlockSpec((ts, width), lambda b, i: (b * nt + i, 0)),
        out_shape=jax.ShapeDtypeStruct((m, width), F32),
        scratch_shapes=[pltpu.VMEM((POOL_HALO + ts, width), F32)],
        compiler_params=_params("parallel", "arbitrary"),
        name="pool_mixer",
    )(u, u, pool_w, pool_scale)


def _gla_constants(c):
    nlev = int(math.log2(c))
    i = np.arange(c)[:, None]
    j = np.arange(c)[None, :]
    x = i ^ j
    level = np.full((c, c), -1, np.int32)
    for lev in range(nlev):
        level = np.where((x >> lev) == 1, lev, level)
    level = np.where(j > i, -1, level)
    level = np.where(i == j, nlev, level)
    return jnp.asarray(j <= i, BF16), jnp.asarray(level, jnp.int32)


def _gla_kernel(z_ref, glr_ref, w2_ref, gb_ref, gn_ref, tri_ref, lev_ref, o_ref, st_ref, cum_ref):
    c = GLA_CHUNK
    nlev = int(math.log2(c))
    ts = z_ref.shape[0]
    sub = 8
    kq, kk, kv, kg = 0, GLA_DK, 2 * GLA_DK, 2 * GLA_DK + GLA_DV

    @pl.when(pl.program_id(2) == 0)
    def _():
        st_ref[...] = jnp.zeros_like(st_ref)

    tri = tri_ref[...]
    level = lev_ref[...]
    gn = gn_ref[...]
    row8 = lax.broadcasted_iota(jnp.int32, (sub, GLA_DK), 0)

    g_all = _log_sigmoid(_dot3(glr_ref[...], w2_ref[...]) + gb_ref[...]) / GLA_TAU
    for ci in range(ts // c):
        cum_ref[pl.ds(ci * c, c), :] = _dot_sel(tri, g_all[ci * c:(ci + 1) * c])

    def cum_row(r):
        return jnp.broadcast_to(cum_ref[pl.ds(r, 1), :], (sub, GLA_DK))

    def level_exponent(ci, lev):
        s = 1 << lev
        tiles = []
        for a in range(c // sub):
            r0 = ci * c + a * sub
            b = cum_ref[pl.ds(r0, sub), :]
            if 2 * s >= sub:
                bm = cum_row(ci * c + (a * sub // (2 * s)) * (2 * s) + s - 1)
            else:
                bm = cum_row(r0 + s - 1)
                for blk in range(2 * s, sub, 2 * s):
                    bm = jnp.where(row8 >= blk, cum_row(r0 + blk + s - 1), bm)
            tiles.append(-jnp.abs(b - bm))
        return jnp.concatenate(tiles, axis=0)

    st = st_ref[...]
    for gi in range(ts // (c * GLA_GROUP)):
        ids = range(gi * GLA_GROUP, (gi + 1) * GLA_GROUP)
        rows = {ci: pl.ds(ci * c, c) for ci in ids}
        q = {ci: z_ref[rows[ci], kq:kq + GLA_DK] * (GLA_DK ** -0.5) for ci in ids}
        k = {ci: z_ref[rows[ci], kk:kk + GLA_DK] for ci in ids}
        b = {ci: cum_ref[rows[ci], :] for ci in ids}
        b_last = {ci: cum_ref[pl.ds(ci * c + c - 1, 1), :] for ci in ids}
        att = {ci: jnp.where(level == nlev, _dot_nt(q[ci], k[ci]), 0.0) for ci in ids}
        for lev in range(nlev):
            el = {ci: jnp.exp(level_exponent(ci, lev)) for ci in ids}
            att = {ci: jnp.where(level == lev, _dot_nt(q[ci] * el[ci], k[ci] * el[ci]), att[ci]) for ci in ids}
        intra = {ci: _dot(att[ci], z_ref[rows[ci], kv:kv + GLA_DV]) for ci in ids}
        qe = {ci: q[ci] * jnp.exp(b[ci]) for ci in ids}
        ke = {ci: k[ci] * jnp.exp(b_last[ci] - b[ci]) for ci in ids}
        for ci in ids:
            o = _dot_nt(qe[ci], st) + intra[ci]
            st = st * jnp.exp(b_last[ci]) + _dot_tn(z_ref[rows[ci], kv:kv + GLA_DV], ke[ci])
            o = o * lax.rsqrt(jnp.mean(o * o, axis=-1, keepdims=True) + NORM_EPS) * gn
            gate = z_ref[rows[ci], kg:kg + GLA_DV]
            o_ref[rows[ci], :] = o * (gate * _sigmoid(gate))
    st_ref[...] = st


def _gla(z, glr, w2p, gate_b, gla_norm, batch):
    h, m, zw = z.shape
    s = m // batch
    ts = min(GLA_TS, s)
    nt = s // ts
    tri, level = _gla_constants(GLA_CHUNK)
    row = lambda b, hh, i: b * nt + i
    return pl.pallas_call(
        _gla_kernel,
        grid=(batch, h, nt),
        in_specs=[
            pl.BlockSpec((None, ts, zw), lambda b, hh, i: (hh, row(b, hh, i), 0)),
            pl.BlockSpec((ts, LANES), lambda b, hh, i: (row(b, hh, i), 0)),
            pl.BlockSpec((None, LANES, GLA_DK), lambda b, hh, i: (hh, 0, 0)),
            pl.BlockSpec((None, 1, GLA_DK), lambda b, hh, i: (hh, 0, 0)),
            pl.BlockSpec((1, GLA_DV), lambda b, hh, i: (0, 0)),
            pl.BlockSpec(tri.shape, lambda b, hh, i: (0, 0)),
            pl.BlockSpec(level.shape, lambda b, hh, i: (0, 0)),
        ],
        out_specs=pl.BlockSpec((ts, GLA_DV), lambda b, hh, i: (row(b, hh, i), hh)),
        out_shape=jax.ShapeDtypeStruct((m, GLA_WIDTH), F32),
        scratch_shapes=[pltpu.VMEM((GLA_DV, GLA_DK), F32), pltpu.VMEM((ts, GLA_DK), F32)],
        compiler_params=_params("parallel", "parallel", "arbitrary"),
        name="gla",
    )(z, glr, w2p, gate_b, gla_norm, tri, level)


def _head_sum_matrix():
    lane = np.arange(LANES)
    return jnp.asarray((lane[:, None] // RWKV_HEAD) == (lane[None, :] // RWKV_HEAD), BF16)


def _shifted(cur, prev8, first):
    prev_row = jnp.where(first, 0.0, prev8[7:8, :])
    rolled = pltpu.roll(cur, 1, 0)
    row = lax.broadcasted_iota(jnp.int32, cur.shape, 0)
    return jnp.where(row == 0, prev_row, rolled)


def _rwkv_prepare(first, rkv_ref, rkvp_ref, lr_ref, lrp_ref, mu_ref, mulr_ref, w0_ref, w2_ref, a0_ref,
                  a2_ref, g2_ref, kk_ref, ka_ref, hs_ref,
                  r_out, lw_out, k_out, v_out, a_out, b_out, g_out):
    def mixed(cur, prev8, mu):
        return cur + (_shifted(cur, prev8, first) - cur) * mu

    r = mixed(rkv_ref[0], rkvp_ref[0], mu_ref[0])
    k = mixed(rkv_ref[1], rkvp_ref[1], mu_ref[1])
    v = mixed(rkv_ref[2], rkvp_ref[2], mu_ref[2])
    lr = mixed(lr_ref[...], lrp_ref[...], mulr_ref[...])

    lr_wa = lr[:, 0:LANES]
    lr_g = lr[:, LANES:RWKV_LR_PAD]
    w_log = _log_sigmoid(w0_ref[...] + _dot(jnp.tanh(lr_wa), w2_ref[...])) - 0.5
    lw = -jnp.exp(w_log)
    a = _sigmoid(a0_ref[...] + _dot(lr_wa, a2_ref[...]))
    g = _dot(_sigmoid(lr_g), g2_ref[...])

    kk = k * kk_ref[...]
    norm = jnp.sqrt(_dot(kk * kk, hs_ref[...]))
    kk = kk / jnp.maximum(norm, 1e-12)
    r_out[...] = r
    lw_out[...] = lw
    k_out[...] = k * (1.0 + (a - 1.0) * ka_ref[...])
    v_out[...] = v
    a_out[...] = -kk
    b_out[...] = kk * a
    g_out[...] = g


def _rwkv_constants(c):
    i = np.arange(c)[:, None]
    j = np.arange(c)[None, :]
    tri = jnp.asarray(j <= i, BF16)
    i2 = np.arange(2 * c)[:, None] % c
    j2 = np.arange(2 * c)[None, :] % c
    strict = jnp.asarray(i2 > j2, F32)
    incl = jnp.asarray(i2 >= j2, F32)
    return tri, strict, incl


def _rwkv_kernel(rkv_ref, rkvp_ref, lr_ref, lrp_ref, mu_ref, mulr_ref, w0_ref, w2_ref, a0_ref, a2_ref, g2_ref,
                 kk_ref, ka_ref, lnw_ref, lnb_ref, rk_ref, tri_ref, strict_ref, incl_ref, hs_ref, o_ref,
                 state_ref, p_ref, gm_ref, rq_ref, y0_ref, r_ref, lw_ref, k_ref, v_ref, a_ref, b_ref, g_ref):
    c = RWKV_CHUNK
    nlev = int(math.log2(c))
    ts = o_ref.shape[0]
    nchunk = ts // c
    first = pl.program_id(2) == 0

    @pl.when(first)
    def _():
        state_ref[...] = jnp.zeros_like(state_ref)

    _rwkv_prepare(first, rkv_ref, rkvp_ref, lr_ref, lrp_ref, mu_ref, mulr_ref, w0_ref, w2_ref, a0_ref, a2_ref,
                  g2_ref, kk_ref, ka_ref, hs_ref, r_ref, lw_ref, k_ref, v_ref, a_ref, b_ref, g_ref)

    lane = lax.broadcasted_iota(jnp.int32, (1, LANES), 1)
    head0 = lane < RWKV_HEAD
    tri = tri_ref[...]
    strict = strict_ref[...]
    incl = incl_ref[...]
    eye = (lax.broadcasted_iota(jnp.int32, (LANES, LANES), 0)
           == lax.broadcasted_iota(jnp.int32, (LANES, LANES), 1)).astype(F32)

    def stack(x):
        return jnp.concatenate([jnp.where(head0, x, 0.0), jnp.where(head0, 0.0, x)], axis=0)

    grp = RWKV_GROUP
    each = range(grp)

    def local(gi):
        base = gi * grp
        rows = [pl.ds((base + t) * c, c) for t in each]
        lw = [lw_ref[rows[t], :] for t in each]
        cum = [_dot_sel(tri, lw[t]) for t in each]
        last = [cum[t][c - 1:c] for t in each]
        grow = [jnp.exp(-cum[t]) for t in each]
        fall = [jnp.exp(last[t] - cum[t]) for t in each]
        at2 = [stack(a_ref[rows[t], :] * jnp.exp(cum[t] - lw[t])) for t in each]
        rt2 = [stack(r_ref[rows[t], :] * jnp.exp(cum[t])) for t in each]
        bt2 = [stack(b_ref[rows[t], :] * grow[t]) for t in each]
        kt2 = [stack(k_ref[rows[t], :] * grow[t]) for t in each]
        bh2 = [stack(b_ref[rows[t], :] * fall[t]) for t in each]
        kh2 = [stack(k_ref[rows[t], :] * fall[t]) for t in each]
        v2 = [stack(v_ref[rows[t], :]) for t in each]
        big = [_dot_nt(jnp.concatenate([at2[t], rt2[t]], axis=0), jnp.concatenate([bt2[t], kt2[t]], axis=0))
               for t in each]
        a_ab = [big[t][0:2 * c, 0:2 * c] * strict for t in each]
        a_ak = [big[t][0:2 * c, 2 * c:4 * c] * strict for t in each]
        a_rb = [big[t][2 * c:4 * c, 0:2 * c] * incl for t in each]
        a_rk = [big[t][2 * c:4 * c, 2 * c:4 * c] * incl for t in each]
        rhs = [jnp.concatenate([at2[t], _dot(a_ak[t], v2[t])], axis=1) for t in each]
        n = a_ab
        for lev in range(nlev):
            rhs = [rhs[t] + _dot(n[t], rhs[t]) for t in each]
            if lev + 1 < nlev:
                n = [_dot(n[t], n[t]) for t in each]
        for t in each:
            w2 = rhs[t][:, 0:LANES]
            z2 = rhs[t][:, LANES:2 * LANES]
            gamma = jnp.exp(last[t])
            p_ref[base + t] = eye * gamma + _dot_tn(w2, bh2[t])
            gm_ref[base + t] = _dot_tn(jnp.concatenate([z2, v2[t]], axis=0),
                                       jnp.concatenate([bh2[t], kh2[t]], axis=0))
            rq_ref[base + t] = rt2[t] + _dot(a_rb[t], w2)
            y0_ref[base + t] = _dot(a_rb[t], z2) + _dot(a_rk[t], v2[t])

    st = state_ref[...]
    for gi in range(nchunk // grp):
        local(gi)
        for ci in range(gi * grp, (gi + 1) * grp):
            y2 = _dot_nt(rq_ref[ci], st) + y0_ref[ci]
            o_ref[pl.ds(ci * c, c), :] = y2[0:c] + y2[c:2 * c]
            st = _dot3(st, p_ref[ci]) + gm_ref[ci]
    state_ref[...] = st

    hs = hs_ref[...]
    y = o_ref[...]
    mean = _dot(y, hs) * (1.0 / RWKV_HEAD)
    yc = y - mean
    var = _dot(yc * yc, hs) * (1.0 / RWKV_HEAD)
    yn = yc * lax.rsqrt(var + RWKV_LN_EPS) * lnw_ref[...] + lnb_ref[...]
    bonus = _dot(r_ref[...] * k_ref[...] * rk_ref[...], hs) * v_ref[...]
    o_ref[...] = (yn + bonus) * g_ref[...]


def _rwkv(z6, lr, mu_rkv, mu_lr, w0, w2p, a0, a2p, g2p, k_k, k_a, ln_w, ln_b, r_k, batch):
    m = lr.shape[0]
    s = m // batch
    ts = min(RWKV_TS, s)
    nt = s // ts
    nb = RWKV_WIDTH // LANES
    c = RWKV_CHUNK
    nchunk = ts // c
    tri, strict, incl = _rwkv_constants(c)
    hs = _head_sum_matrix()
    row = lambda bb, j, i: bb * nt + i
    prow = lambda bb, j, i: jnp.maximum((bb * nt + i) * (ts // 8) - 1, 0)
    vecspec = pl.BlockSpec((1, LANES), lambda bb, j, i: (0, j))
    lrw = pl.BlockSpec((LANES, LANES), lambda bb, j, i: (0, j))
    lrg = pl.BlockSpec((RWKV_LR_PAD - LANES, LANES), lambda bb, j, i: (0, j))
    const = lambda arr: pl.BlockSpec(arr.shape, lambda bb, j, i: (0,) * arr.ndim)
    tile = pltpu.VMEM((ts, LANES), F32)
    return pl.pallas_call(
        _rwkv_kernel,
        grid=(batch, nb, nt),
        in_specs=[
            pl.BlockSpec((3, ts, LANES), lambda bb, j, i: (0, row(bb, j, i), j)),
            pl.BlockSpec((3, 8, LANES), lambda bb, j, i: (0, prow(bb, j, i), j)),
            pl.BlockSpec((ts, RWKV_LR_PAD), lambda bb, j, i: (row(bb, j, i), 0)),
            pl.BlockSpec((8, RWKV_LR_PAD), lambda bb, j, i: (prow(bb, j, i), 0)),
            pl.BlockSpec((3, 1, LANES), lambda bb, j, i: (0, 0, j)),
            pl.BlockSpec((1, RWKV_LR_PAD), lambda bb, j, i: (0, 0)),
            vecspec, lrw, vecspec, lrw, lrg, vecspec, vecspec,
            vecspec, vecspec, vecspec,
            const(tri), const(strict), const(incl), const(hs),
        ],
        out_specs=pl.BlockSpec((ts, LANES), lambda bb, j, i: (row(bb, j, i), j)),
        out_shape=jax.ShapeDtypeStruct((m, RWKV_WIDTH), F32),
        scratch_shapes=[
            pltpu.VMEM((LANES, LANES), F32),
            pltpu.VMEM((nchunk, LANES, LANES), F32),
            pltpu.VMEM((nchunk, LANES, LANES), F32),
            pltpu.VMEM((nchunk, 2 * c, LANES), F32),
            pltpu.VMEM((nchunk, 2 * c, LANES), F32),
        ] + [tile] * 7,
        compiler_params=_params("parallel", "parallel", "arbitrary"),
        name="rwkv",
    )(z6, z6, lr, lr, mu_rkv, mu_lr, w0, w2p, a0, a2p, g2p, k_k, k_a, ln_w, ln_b, r_k, tri, strict, incl, hs)


def _rope(x, cos2, sin2):
    return x * cos2 + pltpu.roll(x, DIL_HEAD // 2, 1) * sin2


def _dil_kernel(q_ref, kp_ref, kc_ref, vp_ref, vc_ref, cos_ref, sin_ref, cosp_ref, sinp_ref, o_ref,
                qs_ref, ks_ref, vs_ref, acc_ref, m_ref, l_ref):
    ts = q_ref.shape[0]
    i = pl.program_id(2)
    blk = DIL_BLOCK
    qs_ref[...] = _rope(q_ref[...], cos_ref[...], sin_ref[...]) * (DIL_HEAD ** -0.5)
    ks_ref[pl.ds(0, ts), :] = _rope(kp_ref[...], cosp_ref[...], sinp_ref[...])
    ks_ref[pl.ds(ts, ts), :] = _rope(kc_ref[...], cos_ref[...], sin_ref[...])
    vs_ref[pl.ds(0, ts), :] = vp_ref[...]
    vs_ref[pl.ds(ts, ts), :] = vc_ref[...]

    qi = lax.broadcasted_iota(jnp.int32, (blk, 2 * blk), 0)
    ki = lax.broadcasted_iota(jnp.int32, (blk, 2 * blk), 1)
    dist = qi + blk - ki
    band = (dist >= 0) & (dist <= blk)

    for pi, (window, dil) in enumerate(DIL_PATTERNS):
        assert window // dil == blk
        per_res = ts // (blk * dil)

        def unit(u, carry, pi=pi, dil=dil, per_res=per_res):
            res = u // per_res
            nblk = u % per_res
            qstart = res + nblk * (blk * dil)
            kstart = ts + qstart - blk * dil
            q = qs_ref[pl.ds(qstart, blk, stride=dil), :]
            k = ks_ref[pl.ds(kstart, 2 * blk, stride=dil), :]
            v = vs_ref[pl.ds(kstart, 2 * blk, stride=dil), :]
            s = _dot_nt(q, k)
            key_pos = i * ts + (qstart - blk * dil) + ki * dil
            s = jnp.where(band & (key_pos >= 0), s, -jnp.inf)
            mx = jnp.max(s, axis=-1, keepdims=True)
            p = jnp.exp(s - mx)
            acc_ref[pi, pl.ds(qstart, blk, stride=dil), :] = _dot(p, v)
            m_ref[pi, pl.ds(qstart, blk, stride=dil), :] = jnp.broadcast_to(mx, (blk, LANES))
            l_ref[pi, pl.ds(qstart, blk, stride=dil), :] = jnp.broadcast_to(
                jnp.sum(p, axis=-1, keepdims=True), (blk, LANES))
            return carry

        lax.fori_loop(0, ts // blk, unit, 0, unroll=8)

    m_all = jnp.maximum(jnp.maximum(m_ref[0], m_ref[1]), m_ref[2])
    num = jnp.zeros((ts, LANES), F32)
    den = jnp.zeros((ts, LANES), F32)
    for pi in range(len(DIL_PATTERNS)):
        cf = jnp.exp(m_ref[pi] - m_all)
        num = num + cf * acc_ref[pi]
        den = den + cf * l_ref[pi]
    o_ref[...] = num / den


def _dilated(z6, cos2, sin2, batch):
    m = z6.shape[1]
    s = m // batch
    ts = DIL_TS
    assert s % ts == 0
    nt = s // ts
    cur = lambda which: pl.BlockSpec((None, ts, DIL_HEAD), lambda b, h, i: (which, b * nt + i, h))
    prev = lambda which: pl.BlockSpec((None, ts, DIL_HEAD),
                                      lambda b, h, i: (which, b * nt + jnp.maximum(i - 1, 0), h))
    tab = pl.BlockSpec((ts, DIL_HEAD), lambda b, h, i: (i, 0))
    tabp = pl.BlockSpec((ts, DIL_HEAD), lambda b, h, i: (jnp.maximum(i - 1, 0), 0))
    npat = len(DIL_PATTERNS)
    return pl.pallas_call(
        _dil_kernel,
        grid=(batch, DIL_HEADS, nt),
        in_specs=[cur(3), prev(4), cur(4), prev(5), cur(5), tab, tab, tabp, tabp],
        out_specs=pl.BlockSpec((ts, DIL_HEAD), lambda b, h, i: (b * nt + i, h)),
        out_shape=jax.ShapeDtypeStruct((m, DIL_WIDTH), F32),
        scratch_shapes=[
            pltpu.VMEM((ts, DIL_HEAD), F32),
            pltpu.VMEM((2 * ts, DIL_HEAD), F32),
            pltpu.VMEM((2 * ts, DIL_HEAD), F32),
            pltpu.VMEM((npat, ts, DIL_HEAD), F32),
            pltpu.VMEM((npat, ts, LANES), F32),
            pltpu.VMEM((npat, ts, LANES), F32),
        ],
        compiler_params=_params("parallel", "parallel", "arbitrary"),
        name="dilated_attention",
    )(z6, z6, z6, z6, z6, cos2, sin2, cos2, sin2)


def _heads(w, n_heads, width):
    d = w.shape[0]
    return w.reshape(d, n_heads, width).transpose(1, 0, 2)


def _pad_rows(w, start, total):
    return jnp.zeros((total, w.shape[1]), w.dtype).at[start:start + w.shape[0]].set(w)


def _even_mixer(x, gain_pre, gain_post, w_in, w_out, pool_w, pool_scale, gate_w2, gate_b, gla_norm, batch):
    d = x.shape[1]
    o0 = POOL_WIDTH
    o1 = o0 + GLA_QK
    o2 = o1 + GLA_QK
    o3 = o2 + GLA_WIDTH
    o4 = o3 + GLA_WIDTH
    wb = w_in.astype(BF16)
    w_u = wb[:, :o0][None]
    w_z = jnp.concatenate([_heads(wb[:, o0:o1], GLA_HEADS, GLA_DK), _heads(wb[:, o1:o2], GLA_HEADS, GLA_DK),
                           _heads(wb[:, o2:o3], GLA_HEADS, GLA_DV), _heads(wb[:, o3:o4], GLA_HEADS, GLA_DV)], 2)
    w_glr = jnp.zeros((d, LANES), BF16).at[:, :GLA_GATE_RANK].set(wb[:, o4:])[None]
    u, z, glr = _proj(x, gain_pre, [w_u, w_z, w_glr])
    u, glr = u[0], glr[0]
    a_out = _pool(u, pool_w.astype(BF16), pool_scale[None], batch)
    w2p = _heads(_pad_rows(gate_w2, 0, LANES), GLA_HEADS, GLA_DK)
    o = _gla(z, glr, w2p, gate_b.reshape(GLA_HEADS, 1, GLA_DK), gla_norm[None], batch)
    wo = w_out.astype(BF16)
    return _mix_out(a_out, o, wo[:POOL_WIDTH], wo[POOL_WIDTH:], x, gain_post)


def _odd_mixer(x, gain_pre, gain_post, w_in, w_out, mu, w0, w2, a0, a2, g2, k_k, k_a, r_k, ln_w, ln_b,
               cos2, sin2, batch):
    d = x.shape[1]
    wb = w_in.astype(BF16)
    c3 = 3 * RWKV_WIDTH
    rwkv_in = c3 + RWKV_LR
    w6 = jnp.concatenate([_heads(wb[:, :c3], 3, RWKV_WIDTH), _heads(wb[:, rwkv_in:], 3, DIL_WIDTH)], 0)
    w_lr = jnp.zeros((d, RWKV_LR_PAD), BF16).at[:, :RWKV_LR].set(wb[:, c3:rwkv_in])[None]
    z6, lr = _proj(x, gain_pre, [w6, w_lr])
    lr = lr[0]
    mu_rkv = mu[:c3].reshape(3, 1, RWKV_WIDTH)
    mu_lr = jnp.zeros((1, RWKV_LR_PAD), F32).at[0, :RWKV_LR].set(mu[c3:])
    w2p = _pad_rows(w2, 0, LANES).astype(BF16)
    a2p = _pad_rows(a2, RWKV_DECAY_RANK, LANES).astype(BF16)
    g2p = _pad_rows(g2, 0, RWKV_LR_PAD - LANES).astype(BF16)
    c_out = _rwkv(z6, lr, mu_rkv, mu_lr, w0[None], w2p, a0[None], a2p, g2p, k_k[None], k_a[None],
                  ln_w[None], ln_b[None], r_k.reshape(1, RWKV_WIDTH), batch)
    d_out = _dilated(z6, cos2, sin2, batch)
    wo = w_out.astype(BF16)
    return _mix_out(c_out, d_out, wo[:RWKV_WIDTH], wo[RWKV_WIDTH:], x, gain_post)


def _rope_tables(s):
    half = DIL_HEAD // 2
    inv = ROPE_THETA ** (-jnp.arange(half, dtype=F32) / half)
    ang = jnp.arange(s).astype(F32)[:, None] * inv[None, :]
    cos, sin = jnp.cos(ang), jnp.sin(ang)
    return jnp.concatenate([cos, cos], axis=-1), jnp.concatenate([-sin, sin], axis=-1)


def kernel(x, p, norm_mix_pre, norm_mix_post, norm_ffn_pre, norm_ffn_post, ev_w_in, ev_w_out, pool_w, pool_scale, gla_gate_w2, gla_gate_b, gla_norm, od_w_in, od_w_out, rwkv_mu, rwkv_w0, rwkv_w2, rwkv_a0, rwkv_a2, rwkv_g2, rwkv_k_k, rwkv_k_a, rwkv_r_k, rwkv_ln_w, rwkv_ln_b, ffn_up, ffn_down, ple_proj, ple_gate, ple_norm):
    batch, s, d = x.shape
    depth = p.shape[0]
    m = batch * s
    xf = x.reshape(m, d)
    cos2, sin2 = _rope_tables(s)
    for i in range(depth):
        j = i // 2
        if i % 2 == 0:
            xf = _even_mixer(xf, norm_mix_pre[i][None], norm_mix_post[i][None], ev_w_in[j], ev_w_out[j],
                             pool_w[j], pool_scale[j], gla_gate_w2[j], gla_gate_b[j], gla_norm[j], batch)
        else:
            xf = _odd_mixer(xf, norm_mix_pre[i][None], norm_mix_post[i][None], od_w_in[j], od_w_out[j],
                            rwkv_mu[j], rwkv_w0[j], rwkv_w2[j], rwkv_a0[j], rwkv_a2[j], rwkv_g2[j],
                            rwkv_k_k[j], rwkv_k_a[j], rwkv_r_k[j], rwkv_ln_w[j], rwkv_ln_b[j],
                            cos2, sin2, batch)
        xf = _ffn(xf, norm_ffn_pre[i][None], norm_ffn_post[i][None],
                  ffn_up[i].astype(BF16), ffn_down[i].astype(BF16))
        xf = _ple(xf, ple_norm[i][None], ple_gate[i].astype(BF16), p[i].reshape(m, PLE_DIM),
                  ple_proj[i].astype(BF16))
    return xf.reshape(batch, s, d)
```

```python
import functools
import math

import numpy as np
import jax
import jax.numpy as jnp
from jax import lax
from jax.experimental import pallas as pl
from jax.experimental.pallas import tpu as pltpu

F32 = jnp.float32
BF16 = jnp.bfloat16
MIX_DTYPE = BF16

NORM_EPS = 1e-6
LANES = 128
VMEM_LIMIT_BYTES = 56 * 1024 * 1024

D_MODEL = 2048
PLE_DIM = 256
POOL_WINDOWS = (2, 4, 8, 16)
POOL_GROUP = 128
POOL_WIDTH = 512
POOL_HALO = 16
GLA_HEADS = 4
GLA_DK = 192
GLA_DV = 384
GLA_QK = GLA_HEADS * GLA_DK
GLA_WIDTH = GLA_HEADS * GLA_DV
GLA_GATE_RANK = 16
GLA_TAU = 16.0
RWKV_HEAD = 64
RWKV_WIDTH = 1024
RWKV_DECAY_RANK = 64
RWKV_A_RANK = 64
RWKV_GATE_RANK = 160
RWKV_LR = RWKV_DECAY_RANK + RWKV_A_RANK + RWKV_GATE_RANK
RWKV_LR_PAD = 384
RWKV_LN_EPS = 64e-5
DIL_WIDTH = 1024
DIL_HEAD = 128
DIL_HEADS = 8
DIL_PATTERNS = ((128, 1), (512, 4), (2048, 16))
DIL_BLOCK = 128
ROPE_THETA = 10000.0

ROW_CHUNK = 256
PROJ_TM = 1024
MIX_TM = 1024
FFN_TM = 512
FFN_TF = 1024
PLE_TM = 1024
POOL_TS = 512
GLA_CHUNK = 64
GLA_TS = 1024
GLA_GROUP = 8
RWKV_CHUNK = 64
RWKV_TS = 1024
RWKV_GROUP = 8
DIL_TS = 2048


def _params(*sem):
    return pltpu.CompilerParams(dimension_semantics=sem, vmem_limit_bytes=VMEM_LIMIT_BYTES)


def _dot(a, b):
    return jnp.dot(a.astype(BF16), b.astype(BF16), preferred_element_type=F32)


def _dot_nt(a, b):
    return lax.dot_general(a.astype(BF16), b.astype(BF16), (((1,), (1,)), ((), ())),
                           preferred_element_type=F32)


def _dot_tn(a, b):
    return lax.dot_general(a.astype(BF16), b.astype(BF16), (((0,), (0,)), ((), ())),
                           preferred_element_type=F32)


def _split(x):
    hi = x.astype(BF16)
    return hi, (x - hi.astype(F32)).astype(BF16)


def _dot3_general(a, b, dims):
    ah, al = _split(a)
    bh, bl = _split(b)
    d = lambda u, v: lax.dot_general(u, v, (dims, ((), ())), preferred_element_type=F32)
    return d(ah, bh) + d(ah, bl) + d(al, bh)


def _dot3(a, b):
    return _dot3_general(a, b, ((1,), (0,)))


def _dot3_nt(a, b):
    return _dot3_general(a, b, ((1,), (1,)))


def _dot_sel(sel_bf16, x):
    hi = x.astype(BF16)
    r1 = x - hi.astype(F32)
    mid = r1.astype(BF16)
    lo = (r1 - mid.astype(F32)).astype(BF16)
    out = jnp.dot(sel_bf16, hi, preferred_element_type=F32)
    out = out + jnp.dot(sel_bf16, mid, preferred_element_type=F32)
    return out + jnp.dot(sel_bf16, lo, preferred_element_type=F32)


def _row_chunks(rows):
    step = min(ROW_CHUNK, rows)
    return [pl.ds(r, step) for r in range(0, rows, step)]


def _rms(x, gain):
    return x * lax.rsqrt(jnp.mean(x * x, axis=-1, keepdims=True) + NORM_EPS) * gain


def _sigmoid(x):
    return 1.0 / (1.0 + jnp.exp(-x))


def _log_sigmoid(x):
    return jnp.minimum(x, 0.0) - jnp.log1p(jnp.exp(-jnp.abs(x)))


def _proj_kernel(starts, x_ref, gain_ref, *refs):
    nw = len(starts) - 1
    w_refs, o_refs, h_ref = refs[:nw], refs[nw:2 * nw], refs[2 * nw]
    j = pl.program_id(1)

    @pl.when(j == 0)
    def _():
        for rows in _row_chunks(x_ref.shape[0]):
            h = _rms(x_ref[rows, :], gain_ref[...]).astype(BF16)
            h_ref[rows, :] = h
            o_refs[0][rows, :] = jnp.dot(h, w_refs[0][...], preferred_element_type=F32)

    for k in range(nw):
        @pl.when((j >= max(starts[k], 1)) & (j < starts[k + 1]))
        def _(k=k):
            o_refs[k][...] = jnp.dot(h_ref[...], w_refs[k][...], preferred_element_type=F32)


def _proj(x, gain, ws):
    m, d = x.shape
    tm = min(PROJ_TM, m)
    starts = [0]
    for w in ws:
        starts.append(starts[-1] + w.shape[0])

    def group(k):
        return lambda i, j: jnp.clip(j - starts[k], 0, ws[k].shape[0] - 1)

    w_specs = [pl.BlockSpec((None, d, w.shape[2]), lambda i, j, gk=group(k): (gk(i, j), 0, 0))
               for k, w in enumerate(ws)]
    o_specs = [pl.BlockSpec((None, tm, w.shape[2]), lambda i, j, gk=group(k): (gk(i, j), i, 0))
               for k, w in enumerate(ws)]
    return pl.pallas_call(
        functools.partial(_proj_kernel, tuple(starts)),
        grid=(m // tm, starts[-1]),
        in_specs=[pl.BlockSpec((tm, d), lambda i, j: (i, 0)), pl.BlockSpec((1, d), lambda i, j: (0, 0))] + w_specs,
        out_specs=o_specs,
        out_shape=[jax.ShapeDtypeStruct((w.shape[0], m, w.shape[2]), F32) for w in ws],
        scratch_shapes=[pltpu.VMEM((tm, d), BF16)],
        compiler_params=_params("parallel", "arbitrary"),
        name="norm_proj",
    )(x, gain, *ws)


def _mix_out_kernel(a1_ref, a2_ref, w1_ref, w2_ref, x_ref, gain_ref, o_ref):
    for rows in _row_chunks(x_ref.shape[0]):
        y = _dot(a1_ref[rows, :], w1_ref[...]) + _dot(a2_ref[rows, :], w2_ref[...])
        o_ref[rows, :] = x_ref[rows, :] + _rms(y, gain_ref[...])


def _resident(shape, index_map):
    return pl.BlockSpec(shape, index_map, pipeline_mode=pl.Buffered(1))


def _mix_out(a1, a2, w1, w2, x, gain):
    m, d = x.shape
    k1, k2 = a1.shape[1], a2.shape[1]
    tm = min(MIX_TM, m)
    return pl.pallas_call(
        _mix_out_kernel,
        grid=(m // tm,),
        in_specs=[
            pl.BlockSpec((tm, k1), lambda i: (i, 0)),
            pl.BlockSpec((tm, k2), lambda i: (i, 0)),
            _resident((k1, d), lambda i: (0, 0)),
            _resident((k2, d), lambda i: (0, 0)),
            pl.BlockSpec((tm, d), lambda i: (i, 0)),
            pl.BlockSpec((1, d), lambda i: (0, 0)),
        ],
        out_specs=pl.BlockSpec((tm, d), lambda i: (i, 0)),
        out_shape=jax.ShapeDtypeStruct((m, d), F32),
        compiler_params=_params("parallel"),
        name="mix_out",
    )(a1, a2, w1, w2, x, gain)


def _ffn_kernel(x_ref, gpre_ref, gpost_ref, up_ref, down_ref, o_ref, h_ref, acc_ref):
    f = pl.program_id(1)
    last = pl.num_programs(1) - 1

    def contribution(h):
        a = jnp.maximum(jnp.dot(h, up_ref[...], preferred_element_type=F32), 0.0)
        return jnp.dot((a * a).astype(BF16), down_ref[...], preferred_element_type=F32)

    @pl.when(f == 0)
    def _():
        for rows in _row_chunks(x_ref.shape[0]):
            h = _rms(x_ref[rows, :], gpre_ref[...]).astype(BF16)
            h_ref[rows, :] = h
            acc_ref[rows, :] = contribution(h)

    @pl.when((f > 0) & (f < last))
    def _():
        acc_ref[...] += contribution(h_ref[...])

    @pl.when(f == last)
    def _():
        for rows in _row_chunks(x_ref.shape[0]):
            y = acc_ref[rows, :] + contribution(h_ref[rows, :])
            o_ref[rows, :] = x_ref[rows, :] + _rms(y, gpost_ref[...])


def _ffn(x, gpre, gpost, up, down, layer):
    m, d = x.shape
    dff = up.shape[2]
    tm = min(FFN_TM, m)
    tf = min(FFN_TF, dff)
    assert dff // tf >= 2
    return pl.pallas_call(
        _ffn_kernel,
        grid=(m // tm, dff // tf),
        in_specs=[
            pl.BlockSpec((tm, d), lambda i, f: (i, 0)),
            pl.BlockSpec((1, d), lambda i, f: (0, 0)),
            pl.BlockSpec((1, d), lambda i, f: (0, 0)),
            pl.BlockSpec((None, d, tf), lambda i, f: (layer, 0, f)),
            pl.BlockSpec((None, tf, d), lambda i, f: (layer, f, 0)),
        ],
        out_specs=pl.BlockSpec((tm, d), lambda i, f: (i, 0)),
        out_shape=jax.ShapeDtypeStruct((m, d), F32),
        scratch_shapes=[pltpu.VMEM((tm, d), BF16), pltpu.VMEM((tm, d), F32)],
        compiler_params=_params("parallel", "arbitrary"),
        name="ffn",
    )(x, gpre, gpost, up, down)


def _ple_kernel(x_ref, gain_ref, wg_ref, p_ref, wp_ref, o_ref):
    for rows in _row_chunks(x_ref.shape[0]):
        x = x_ref[rows, :]
        gate = _sigmoid(_dot(_rms(x, gain_ref[...]), wg_ref[...]))
        o_ref[rows, :] = x + _dot(p_ref[rows, :], wp_ref[...]) * gate


def _ple(x, gain, wg, p, wp, layer):
    m, d = x.shape
    pd = p.shape[2]
    tm = min(PLE_TM, m)
    return pl.pallas_call(
        _ple_kernel,
        grid=(m // tm,),
        in_specs=[
            pl.BlockSpec((tm, d), lambda i: (i, 0)),
            pl.BlockSpec((1, d), lambda i: (0, 0)),
            _resident((None, d, d), lambda i: (layer, 0, 0)),
            pl.BlockSpec((None, tm, pd), lambda i: (layer, i, 0)),
            _resident((None, pd, d), lambda i: (layer, 0, 0)),
        ],
        out_specs=pl.BlockSpec((tm, d), lambda i: (i, 0)),
        out_shape=jax.ShapeDtypeStruct((m, d), F32),
        compiler_params=_params("parallel"),
        name="ple",
    )(x, gain, wg, p, wp)


def _pool_kernel(u_ref, prev_ref, w_ref, scale_ref, o_ref, ext_ref):
    i = pl.program_id(1)
    ts = u_ref.shape[0]
    ext_ref[pl.ds(POOL_HALO, ts), :] = u_ref[...]

    @pl.when(i == 0)
    def _():
        ext_ref[pl.ds(0, POOL_HALO), :] = jnp.zeros((POOL_HALO, POOL_WIDTH), F32)

    @pl.when(i > 0)
    def _():
        ext_ref[pl.ds(0, POOL_HALO), :] = prev_ref[...]

    t = i * ts + lax.broadcasted_iota(jnp.int32, (ts, 1), 0)
    for gi, w in enumerate(POOL_WINDOWS):
        lo = gi * POOL_GROUP
        cur = ext_ref[pl.ds(POOL_HALO, ts), pl.ds(lo, POOL_GROUP)]
        tot = cur
        for j in range(1, w):
            tot = tot + ext_ref[pl.ds(POOL_HALO - j, ts), pl.ds(lo, POOL_GROUP)]
        cnt = jnp.minimum(t + 1, w).astype(F32)
        pooled = tot / cnt - cur
        out = _dot(pooled, w_ref[gi]) * scale_ref[:, pl.ds(lo, POOL_GROUP)]
        o_ref[:, pl.ds(lo, POOL_GROUP)] = out.astype(o_ref.dtype)


def _pool(u, pool_w, pool_scale, batch):
    m, width = u.shape
    s = m // batch
    ts = min(POOL_TS, s)
    nt = s // ts
    hb = ts // POOL_HALO
    return pl.pallas_call(
        _pool_kernel,
        grid=(batch, nt),
        in_specs=[
            pl.BlockSpec((ts, width), lambda b, i: (b * nt + i, 0)),
            pl.BlockSpec((POOL_HALO, width), lambda b, i: (jnp.maximum((b * nt + i) * hb - 1, 0), 0)),
            pl.BlockSpec((len(POOL_WINDOWS), POOL_GROUP, POOL_GROUP), lambda b, i: (0, 0, 0)),
            pl.BlockSpec((1, width), lambda b, i: (0, 0)),
        ],
        out_specs=pl.BlockSpec((ts, width), lambda b, i: (b * nt + i, 0)),
        out_shape=jax.ShapeDtypeStruct((m, width), MIX_DTYPE),
        scratch_shapes=[pltpu.VMEM((POOL_HALO + ts, width), F32)],
        compiler_params=_params("parallel", "arbitrary"),
        name="pool_mixer",
    )(u, u, pool_w, pool_scale)


def _gla_constants(c):
    nlev = int(math.log2(c))
    i = np.arange(c)[:, None]
    j = np.arange(c)[None, :]
    x = i ^ j
    level = np.full((c, c), -1, np.int32)
    for lev in range(nlev):
        level = np.where((x >> lev) == 1, lev, level)
    level = np.where(j > i, -1, level)
    level = np.where(i == j, nlev, level)
    return jnp.asarray(j <= i, BF16), jnp.asarray(level, jnp.int32)


def _gla_kernel(z_ref, glr_ref, w2_ref, gb_ref, gn_ref, tri_ref, lev_ref, o_ref, st_ref, cum_ref):
    c = GLA_CHUNK
    nlev = int(math.log2(c))
    ts = z_ref.shape[0]
    sub = 8
    kq, kk, kv, kg = 0, GLA_DK, 2 * GLA_DK, 2 * GLA_DK + GLA_DV

    @pl.when(pl.program_id(2) == 0)
    def _():
        st_ref[...] = jnp.zeros_like(st_ref)

    tri = tri_ref[...]
    level = lev_ref[...]
    gn = gn_ref[...]
    row8 = lax.broadcasted_iota(jnp.int32, (sub, GLA_DK), 0)

    g_all = _log_sigmoid(_dot3(glr_ref[...], w2_ref[...]) + gb_ref[...]) / GLA_TAU
    for ci in range(ts // c):
        cum_ref[pl.ds(ci * c, c), :] = _dot_sel(tri, g_all[ci * c:(ci + 1) * c])

    def cum_row(r):
        return jnp.broadcast_to(cum_ref[pl.ds(r, 1), :], (sub, GLA_DK))

    def level_exponent(ci, lev):
        s = 1 << lev
        tiles = []
        for a in range(c // sub):
            r0 = ci * c + a * sub
            b = cum_ref[pl.ds(r0, sub), :]
            if 2 * s >= sub:
                bm = cum_row(ci * c + (a * sub // (2 * s)) * (2 * s) + s - 1)
            else:
                bm = cum_row(r0 + s - 1)
                for blk in range(2 * s, sub, 2 * s):
                    bm = jnp.where(row8 >= blk, cum_row(r0 + blk + s - 1), bm)
            tiles.append(-jnp.abs(b - bm))
        return jnp.concatenate(tiles, axis=0)

    st = st_ref[...]
    for gi in range(ts // (c * GLA_GROUP)):
        ids = range(gi * GLA_GROUP, (gi + 1) * GLA_GROUP)
        rows = {ci: pl.ds(ci * c, c) for ci in ids}
        q = {ci: z_ref[rows[ci], kq:kq + GLA_DK] * (GLA_DK ** -0.5) for ci in ids}
        k = {ci: z_ref[rows[ci], kk:kk + GLA_DK] for ci in ids}
        b = {ci: cum_ref[rows[ci], :] for ci in ids}
        b_last = {ci: cum_ref[pl.ds(ci * c + c - 1, 1), :] for ci in ids}
        att = {ci: jnp.where(level == nlev, _dot_nt(q[ci], k[ci]), 0.0) for ci in ids}
        for lev in range(nlev):
            el = {ci: jnp.exp(level_exponent(ci, lev)) for ci in ids}
            att = {ci: jnp.where(level == lev, _dot_nt(q[ci] * el[ci], k[ci] * el[ci]), att[ci]) for ci in ids}
        intra = {ci: _dot(att[ci], z_ref[rows[ci], kv:kv + GLA_DV]) for ci in ids}
        qe = {ci: q[ci] * jnp.exp(b[ci]) for ci in ids}
        ke = {ci: k[ci] * jnp.exp(b_last[ci] - b[ci]) for ci in ids}
        for ci in ids:
            o = _dot_nt(qe[ci], st) + intra[ci]
            st = st * jnp.exp(b_last[ci]) + _dot_tn(z_ref[rows[ci], kv:kv + GLA_DV], ke[ci])
            o = o * lax.rsqrt(jnp.mean(o * o, axis=-1, keepdims=True) + NORM_EPS) * gn
            gate = z_ref[rows[ci], kg:kg + GLA_DV]
            o_ref[rows[ci], :] = (o * (gate * _sigmoid(gate))).astype(o_ref.dtype)
    st_ref[...] = st


def _gla(z, glr, w2p, gate_b, gla_norm, batch):
    h, m, zw = z.shape
    s = m // batch
    ts = min(GLA_TS, s)
    nt = s // ts
    tri, level = _gla_constants(GLA_CHUNK)
    row = lambda b, hh, i: b * nt + i
    return pl.pallas_call(
        _gla_kernel,
        grid=(batch, h, nt),
        in_specs=[
            pl.BlockSpec((None, ts, zw), lambda b, hh, i: (hh, row(b, hh, i), 0)),
            pl.BlockSpec((ts, LANES), lambda b, hh, i: (row(b, hh, i), 0)),
            pl.BlockSpec((None, LANES, GLA_DK), lambda b, hh, i: (hh, 0, 0)),
            pl.BlockSpec((None, 1, GLA_DK), lambda b, hh, i: (hh, 0, 0)),
            pl.BlockSpec((1, GLA_DV), lambda b, hh, i: (0, 0)),
            pl.BlockSpec(tri.shape, lambda b, hh, i: (0, 0)),
            pl.BlockSpec(level.shape, lambda b, hh, i: (0, 0)),
        ],
        out_specs=pl.BlockSpec((ts, GLA_DV), lambda b, hh, i: (row(b, hh, i), hh)),
        out_shape=jax.ShapeDtypeStruct((m, GLA_WIDTH), MIX_DTYPE),
        scratch_shapes=[pltpu.VMEM((GLA_DV, GLA_DK), F32), pltpu.VMEM((ts, GLA_DK), F32)],
        compiler_params=_params("parallel", "parallel", "arbitrary"),
        name="gla",
    )(z, glr, w2p, gate_b, gla_norm, tri, level)


def _head_sum_matrix():
    lane = np.arange(LANES)
    return jnp.asarray((lane[:, None] // RWKV_HEAD) == (lane[None, :] // RWKV_HEAD), BF16)


def _shifted(cur, prev8, first):
    prev_row = jnp.where(first, 0.0, prev8[7:8, :])
    rolled = pltpu.roll(cur, 1, 0)
    row = lax.broadcasted_iota(jnp.int32, cur.shape, 0)
    return jnp.where(row == 0, prev_row, rolled)


def _rwkv_prepare(first, rkv_ref, rkvp_ref, lr_ref, lrp_ref, mu_ref, mulr_ref, w0_ref, w2_ref, a0_ref,
                  a2_ref, g2_ref, kk_ref, ka_ref, hs_ref,
                  r_out, lw_out, k_out, v_out, a_out, b_out, g_out):
    def mixed(cur, prev8, mu):
        return cur + (_shifted(cur, prev8, first) - cur) * mu

    r = mixed(rkv_ref[0], rkvp_ref[0], mu_ref[0])
    k = mixed(rkv_ref[1], rkvp_ref[1], mu_ref[1])
    v = mixed(rkv_ref[2], rkvp_ref[2], mu_ref[2])
    lr = mixed(lr_ref[...], lrp_ref[...], mulr_ref[...])

    lr_wa = lr[:, 0:LANES]
    lr_g = lr[:, LANES:RWKV_LR_PAD]
    w_log = _log_sigmoid(w0_ref[...] + _dot(jnp.tanh(lr_wa), w2_ref[...])) - 0.5
    lw = -jnp.exp(w_log)
    a = _sigmoid(a0_ref[...] + _dot(lr_wa, a2_ref[...]))
    g = _dot(_sigmoid(lr_g), g2_ref[...])

    kk = k * kk_ref[...]
    norm = jnp.sqrt(_dot(kk * kk, hs_ref[...]))
    kk = kk / jnp.maximum(norm, 1e-12)
    r_out[...] = r
    lw_out[...] = lw
    k_out[...] = k * (1.0 + (a - 1.0) * ka_ref[...])
    v_out[...] = v
    a_out[...] = -kk
    b_out[...] = kk * a
    g_out[...] = g


def _rwkv_constants(c):
    i = np.arange(c)[:, None]
    j = np.arange(c)[None, :]
    tri = jnp.asarray(j <= i, BF16)
    i2 = np.arange(2 * c)[:, None] % c
    j2 = np.arange(2 * c)[None, :] % c
    strict = jnp.asarray(i2 > j2, F32)
    incl = jnp.asarray(i2 >= j2, F32)
    return tri, strict, incl


def _rwkv_kernel(rkv_ref, rkvp_ref, lr_ref, lrp_ref, mu_ref, mulr_ref, w0_ref, w2_ref, a0_ref, a2_ref, g2_ref,
                 kk_ref, ka_ref, lnw_ref, lnb_ref, rk_ref, tri_ref, strict_ref, incl_ref, hs_ref, o_ref,
                 state_ref, p_ref, gm_ref, rq_ref, y0_ref, r_ref, lw_ref, k_ref, v_ref, a_ref, b_ref, g_ref, y_ref):
    c = RWKV_CHUNK
    nlev = int(math.log2(c))
    ts = o_ref.shape[0]
    nchunk = ts // c
    first = pl.program_id(2) == 0

    @pl.when(first)
    def _():
        state_ref[...] = jnp.zeros_like(state_ref)

    _rwkv_prepare(first, rkv_ref, rkvp_ref, lr_ref, lrp_ref, mu_ref, mulr_ref, w0_ref, w2_ref, a0_ref, a2_ref,
                  g2_ref, kk_ref, ka_ref, hs_ref, r_ref, lw_ref, k_ref, v_ref, a_ref, b_ref, g_ref)

    lane = lax.broadcasted_iota(jnp.int32, (1, LANES), 1)
    head0 = lane < RWKV_HEAD
    tri = tri_ref[...]
    strict = strict_ref[...]
    incl = incl_ref[...]
    eye = (lax.broadcasted_iota(jnp.int32, (LANES, LANES), 0)
           == lax.broadcasted_iota(jnp.int32, (LANES, LANES), 1)).astype(F32)

    def stack(x):
        return jnp.concatenate([jnp.where(head0, x, 0.0), jnp.where(head0, 0.0, x)], axis=0)

    grp = RWKV_GROUP
    each = range(grp)

    def local(gi):
        base = gi * grp
        rows = [pl.ds((base + t) * c, c) for t in each]
        lw = [lw_ref[rows[t], :] for t in each]
        cum = [_dot_sel(tri, lw[t]) for t in each]
        last = [cum[t][c - 1:c] for t in each]
        grow = [jnp.exp(-cum[t]) for t in each]
        fall = [jnp.exp(last[t] - cum[t]) for t in each]
        at2 = [stack(a_ref[rows[t], :] * jnp.exp(cum[t] - lw[t])) for t in each]
        rt2 = [stack(r_ref[rows[t], :] * jnp.exp(cum[t])) for t in each]
        bt2 = [stack(b_ref[rows[t], :] * grow[t]) for t in each]
        kt2 = [stack(k_ref[rows[t], :] * grow[t]) for t in each]
        bh2 = [stack(b_ref[rows[t], :] * fall[t]) for t in each]
        kh2 = [stack(k_ref[rows[t], :] * fall[t]) for t in each]
        v2 = [stack(v_ref[rows[t], :]) for t in each]
        big = [_dot_nt(jnp.concatenate([at2[t], rt2[t]], axis=0), jnp.concatenate([bt2[t], kt2[t]], axis=0))
               for t in each]
        a_ab = [big[t][0:2 * c, 0:2 * c] * strict for t in each]
        a_ak = [big[t][0:2 * c, 2 * c:4 * c] * strict for t in each]
        a_rb = [big[t][2 * c:4 * c, 0:2 * c] * incl for t in each]
        a_rk = [big[t][2 * c:4 * c, 2 * c:4 * c] * incl for t in each]
        rhs = [jnp.concatenate([at2[t], _dot(a_ak[t], v2[t])], axis=1) for t in each]
        n = a_ab
        for lev in range(nlev):
            rhs = [rhs[t] + _dot(n[t], rhs[t]) for t in each]
            if lev + 1 < nlev:
                n = [_dot(n[t], n[t]) for t in each]
        for t in each:
            w2 = rhs[t][:, 0:LANES]
            z2 = rhs[t][:, LANES:2 * LANES]
            gamma = jnp.exp(last[t])
            p_ref[base + t] = eye * gamma + _dot_tn(w2, bh2[t])
            gm_ref[base + t] = _dot_tn(jnp.concatenate([z2, v2[t]], axis=0),
                                       jnp.concatenate([bh2[t], kh2[t]], axis=0))
            rq_ref[base + t] = rt2[t] + _dot(a_rb[t], w2)
            y0_ref[base + t] = _dot(a_rb[t], z2) + _dot(a_rk[t], v2[t])

    st = state_ref[...]
    for gi in range(nchunk // grp):
        local(gi)
        for ci in range(gi * grp, (gi + 1) * grp):
            y2 = _dot_nt(rq_ref[ci], st) + y0_ref[ci]
            y_ref[pl.ds(ci * c, c), :] = y2[0:c] + y2[c:2 * c]
            st = _dot3(st, p_ref[ci]) + gm_ref[ci]
    state_ref[...] = st

    hs = hs_ref[...]
    y = y_ref[...]
    mean = _dot(y, hs) * (1.0 / RWKV_HEAD)
    yc = y - mean
    var = _dot(yc * yc, hs) * (1.0 / RWKV_HEAD)
    yn = yc * lax.rsqrt(var + RWKV_LN_EPS) * lnw_ref[...] + lnb_ref[...]
    bonus = _dot(r_ref[...] * k_ref[...] * rk_ref[...], hs) * v_ref[...]
    o_ref[...] = ((yn + bonus) * g_ref[...]).astype(o_ref.dtype)


def _rwkv(z6, lr, mu_rkv, mu_lr, w0, w2p, a0, a2p, g2p, k_k, k_a, ln_w, ln_b, r_k, batch):
    m = lr.shape[0]
    s = m // batch
    ts = min(RWKV_TS, s)
    nt = s // ts
    nb = RWKV_WIDTH // LANES
    c = RWKV_CHUNK
    nchunk = ts // c
    tri, strict, incl = _rwkv_constants(c)
    hs = _head_sum_matrix()
    row = lambda bb, j, i: bb * nt + i
    prow = lambda bb, j, i: jnp.maximum((bb * nt + i) * (ts // 8) - 1, 0)
    vecspec = pl.BlockSpec((1, LANES), lambda bb, j, i: (0, j))
    lrw = pl.BlockSpec((LANES, LANES), lambda bb, j, i: (0, j))
    lrg = pl.BlockSpec((RWKV_LR_PAD - LANES, LANES), lambda bb, j, i: (0, j))
    const = lambda arr: pl.BlockSpec(arr.shape, lambda bb, j, i: (0,) * arr.ndim)
    tile = pltpu.VMEM((ts, LANES), F32)
    return pl.pallas_call(
        _rwkv_kernel,
        grid=(batch, nb, nt),
        in_specs=[
            pl.BlockSpec((3, ts, LANES), lambda bb, j, i: (0, row(bb, j, i), j)),
            pl.BlockSpec((3, 8, LANES), lambda bb, j, i: (0, prow(bb, j, i), j)),
            pl.BlockSpec((ts, RWKV_LR_PAD), lambda bb, j, i: (row(bb, j, i), 0)),
            pl.BlockSpec((8, RWKV_LR_PAD), lambda bb, j, i: (prow(bb, j, i), 0)),
            pl.BlockSpec((3, 1, LANES), lambda bb, j, i: (0, 0, j)),
            pl.BlockSpec((1, RWKV_LR_PAD), lambda bb, j, i: (0, 0)),
            vecspec, lrw, vecspec, lrw, lrg, vecspec, vecspec,
            vecspec, vecspec, vecspec,
            const(tri), const(strict), const(incl), const(hs),
        ],
        out_specs=pl.BlockSpec((ts, LANES), lambda bb, j, i: (row(bb, j, i), j)),
        out_shape=jax.ShapeDtypeStruct((m, RWKV_WIDTH), MIX_DTYPE),
        scratch_shapes=[
            pltpu.VMEM((LANES, LANES), F32),
            pltpu.VMEM((nchunk, LANES, LANES), F32),
            pltpu.VMEM((nchunk, LANES, LANES), F32),
            pltpu.VMEM((nchunk, 2 * c, LANES), F32),
            pltpu.VMEM((nchunk, 2 * c, LANES), F32),
        ] + [tile] * 8,
        compiler_params=_params("parallel", "parallel", "arbitrary"),
        name="rwkv",
    )(z6, z6, lr, lr, mu_rkv, mu_lr, w0, w2p, a0, a2p, g2p, k_k, k_a, ln_w, ln_b, r_k, tri, strict, incl, hs)


def _rope(x, cos2, sin2):
    return x * cos2 + pltpu.roll(x, DIL_HEAD // 2, 1) * sin2


def _dil_kernel(q_ref, kp_ref, kc_ref, vp_ref, vc_ref, cos_ref, sin_ref, cosp_ref, sinp_ref, o_ref,
                qs_ref, ks_ref, vs_ref, acc_ref, m_ref, l_ref):
    ts = q_ref.shape[0]
    i = pl.program_id(2)
    blk = DIL_BLOCK
    qs_ref[...] = _rope(q_ref[...], cos_ref[...], sin_ref[...]) * (DIL_HEAD ** -0.5)
    ks_ref[pl.ds(0, ts), :] = _rope(kp_ref[...], cosp_ref[...], sinp_ref[...])
    ks_ref[pl.ds(ts, ts), :] = _rope(kc_ref[...], cos_ref[...], sin_ref[...])
    vs_ref[pl.ds(0, ts), :] = vp_ref[...]
    vs_ref[pl.ds(ts, ts), :] = vc_ref[...]

    qi = lax.broadcasted_iota(jnp.int32, (blk, 2 * blk), 0)
    ki = lax.broadcasted_iota(jnp.int32, (blk, 2 * blk), 1)
    dist = qi + blk - ki
    band = (dist >= 0) & (dist <= blk)

    for pi, (window, dil) in enumerate(DIL_PATTERNS):
        assert window // dil == blk
        per_res = ts // (blk * dil)

        def unit(u, carry, pi=pi, dil=dil, per_res=per_res):
            res = u // per_res
            nblk = u % per_res
            qstart = res + nblk * (blk * dil)
            kstart = ts + qstart - blk * dil
            q = qs_ref[pl.ds(qstart, blk, stride=dil), :]
            k = ks_ref[pl.ds(kstart, 2 * blk, stride=dil), :]
            v = vs_ref[pl.ds(kstart, 2 * blk, stride=dil), :]
            s = _dot_nt(q, k)
            key_pos = i * ts + (qstart - blk * dil) + ki * dil
            s = jnp.where(band & (key_pos >= 0), s, -jnp.inf)
            mx = jnp.max(s, axis=-1, keepdims=True)
            p = jnp.exp(s - mx)
            acc_ref[pi, pl.ds(qstart, blk, stride=dil), :] = _dot(p, v)
            m_ref[pi, pl.ds(qstart, blk, stride=dil), :] = jnp.broadcast_to(mx, (blk, LANES))
            l_ref[pi, pl.ds(qstart, blk, stride=dil), :] = jnp.broadcast_to(
                jnp.sum(p, axis=-1, keepdims=True), (blk, LANES))
            return carry

        lax.fori_loop(0, ts // blk, unit, 0, unroll=8)

    m_all = jnp.maximum(jnp.maximum(m_ref[0], m_ref[1]), m_ref[2])
    num = jnp.zeros((ts, LANES), F32)
    den = jnp.zeros((ts, LANES), F32)
    for pi in range(len(DIL_PATTERNS)):
        cf = jnp.exp(m_ref[pi] - m_all)
        num = num + cf * acc_ref[pi]
        den = den + cf * l_ref[pi]
    o_ref[...] = (num / den).astype(o_ref.dtype)


def _dilated(z6, cos2, sin2, batch):
    m = z6.shape[1]
    s = m // batch
    ts = DIL_TS
    assert s % ts == 0
    nt = s // ts
    cur = lambda which: pl.BlockSpec((None, ts, DIL_HEAD), lambda b, h, i: (which, b * nt + i, h))
    prev = lambda which: pl.BlockSpec((None, ts, DIL_HEAD),
                                      lambda b, h, i: (which, b * nt + jnp.maximum(i - 1, 0), h))
    tab = pl.BlockSpec((ts, DIL_HEAD), lambda b, h, i: (i, 0))
    tabp = pl.BlockSpec((ts, DIL_HEAD), lambda b, h, i: (jnp.maximum(i - 1, 0), 0))
    npat = len(DIL_PATTERNS)
    return pl.pallas_call(
        _dil_kernel,
        grid=(batch, DIL_HEADS, nt),
        in_specs=[cur(3), prev(4), cur(4), prev(5), cur(5), tab, tab, tabp, tabp],
        out_specs=pl.BlockSpec((ts, DIL_HEAD), lambda b, h, i: (b * nt + i, h)),
        out_shape=jax.ShapeDtypeStruct((m, DIL_WIDTH), MIX_DTYPE),
        scratch_shapes=[
            pltpu.VMEM((ts, DIL_HEAD), F32),
            pltpu.VMEM((2 * ts, DIL_HEAD), F32),
            pltpu.VMEM((2 * ts, DIL_HEAD), F32),
            pltpu.VMEM((npat, ts, DIL_HEAD), F32),
            pltpu.VMEM((npat, ts, LANES), F32),
            pltpu.VMEM((npat, ts, LANES), F32),
        ],
        compiler_params=_params("parallel", "parallel", "arbitrary"),
        name="dilated_attention",
    )(z6, z6, z6, z6, z6, cos2, sin2, cos2, sin2)


def _heads(w, n_heads, width):
    d = w.shape[0]
    return w.reshape(d, n_heads, width).transpose(1, 0, 2)


def _pad_rows(w, start, total):
    return jnp.zeros((total, w.shape[1]), w.dtype).at[start:start + w.shape[0]].set(w)


def _even_mixer(x, gain_pre, gain_post, w_in, w_out, pool_w, pool_scale, gate_w2, gate_b, gla_norm, batch):
    d = x.shape[1]
    o0 = POOL_WIDTH
    o1 = o0 + GLA_QK
    o2 = o1 + GLA_QK
    o3 = o2 + GLA_WIDTH
    o4 = o3 + GLA_WIDTH
    wb = w_in.astype(BF16)
    w_u = wb[:, :o0][None]
    w_z = jnp.concatenate([_heads(wb[:, o0:o1], GLA_HEADS, GLA_DK), _heads(wb[:, o1:o2], GLA_HEADS, GLA_DK),
                           _heads(wb[:, o2:o3], GLA_HEADS, GLA_DV), _heads(wb[:, o3:o4], GLA_HEADS, GLA_DV)], 2)
    w_glr = jnp.zeros((d, LANES), BF16).at[:, :GLA_GATE_RANK].set(wb[:, o4:])[None]
    u, z, glr = _proj(x, gain_pre, [w_u, w_z, w_glr])
    u, glr = u[0], glr[0]
    a_out = _pool(u, pool_w.astype(BF16), pool_scale[None], batch)
    w2p = _heads(_pad_rows(gate_w2, 0, LANES), GLA_HEADS, GLA_DK)
    o = _gla(z, glr, w2p, gate_b.reshape(GLA_HEADS, 1, GLA_DK), gla_norm[None], batch)
    wo = w_out.astype(BF16)
    return _mix_out(a_out, o, wo[:POOL_WIDTH], wo[POOL_WIDTH:], x, gain_post)


def _odd_mixer(x, gain_pre, gain_post, w_in, w_out, mu, w0, w2, a0, a2, g2, k_k, k_a, r_k, ln_w, ln_b,
               cos2, sin2, batch):
    d = x.shape[1]
    wb = w_in.astype(BF16)
    c3 = 3 * RWKV_WIDTH
    rwkv_in = c3 + RWKV_LR
    w6 = jnp.concatenate([_heads(wb[:, :c3], 3, RWKV_WIDTH), _heads(wb[:, rwkv_in:], 3, DIL_WIDTH)], 0)
    w_lr = jnp.zeros((d, RWKV_LR_PAD), BF16).at[:, :RWKV_LR].set(wb[:, c3:rwkv_in])[None]
    z6, lr = _proj(x, gain_pre, [w6, w_lr])
    lr = lr[0]
    mu_rkv = mu[:c3].reshape(3, 1, RWKV_WIDTH)
    mu_lr = jnp.zeros((1, RWKV_LR_PAD), F32).at[0, :RWKV_LR].set(mu[c3:])
    w2p = _pad_rows(w2, 0, LANES).astype(BF16)
    a2p = _pad_rows(a2, RWKV_DECAY_RANK, LANES).astype(BF16)
    g2p = _pad_rows(g2, 0, RWKV_LR_PAD - LANES).astype(BF16)
    c_out = _rwkv(z6, lr, mu_rkv, mu_lr, w0[None], w2p, a0[None], a2p, g2p, k_k[None], k_a[None],
                  ln_w[None], ln_b[None], r_k.reshape(1, RWKV_WIDTH), batch)
    d_out = _dilated(z6, cos2, sin2, batch)
    wo = w_out.astype(BF16)
    return _mix_out(c_out, d_out, wo[:RWKV_WIDTH], wo[RWKV_WIDTH:], x, gain_post)


def _rope_tables(s):
    half = DIL_HEAD // 2
    inv = ROPE_THETA ** (-jnp.arange(half, dtype=F32) / half)
    ang = jnp.arange(s).astype(F32)[:, None] * inv[None, :]
    cos, sin = jnp.cos(ang), jnp.sin(ang)
    return jnp.concatenate([cos, cos], axis=-1), jnp.concatenate([-sin, sin], axis=-1)


def kernel(x, p, norm_mix_pre, norm_mix_post, norm_ffn_pre, norm_ffn_post, ev_w_in, ev_w_out, pool_w, pool_scale, gla_gate_w2, gla_gate_b, gla_norm, od_w_in, od_w_out, rwkv_mu, rwkv_w0, rwkv_w2, rwkv_a0, rwkv_a2, rwkv_g2, rwkv_k_k, rwkv_k_a, rwkv_r_k, rwkv_ln_w, rwkv_ln_b, ffn_up, ffn_down, ple_proj, ple_gate, ple_norm):
    batch, s, d = x.shape
    depth = p.shape[0]
    m = batch * s
    xf = x.reshape(m, d)
    cos2, sin2 = _rope_tables(s)
    up_all, down_all = ffn_up.astype(BF16), ffn_down.astype(BF16)
    gate_all, proj_all = ple_gate.astype(BF16), ple_proj.astype(BF16)
    p_all = p.reshape(depth, m, PLE_DIM)
    for i in range(depth):
        j = i // 2
        if i % 2 == 0:
            xf = _even_mixer(xf, norm_mix_pre[i][None], norm_mix_post[i][None], ev_w_in[j], ev_w_out[j],
                             pool_w[j], pool_scale[j], gla_gate_w2[j], gla_gate_b[j], gla_norm[j], batch)
        else:
            xf = _odd_mixer(xf, norm_mix_pre[i][None], norm_mix_post[i][None], od_w_in[j], od_w_out[j],
                            rwkv_mu[j], rwkv_w0[j], rwkv_w2[j], rwkv_a0[j], rwkv_a2[j], rwkv_g2[j],
                            rwkv_k_k[j], rwkv_k_a[j], rwkv_r_k[j], rwkv_ln_w[j], rwkv_ln_b[j],
                            cos2, sin2, batch)
        xf = _ffn(xf, norm_ffn_pre[i][None], norm_ffn_post[i][None], up_all, down_all, i)
        xf = _ple(xf, ple_norm[i][None], gate_all, p_all, proj_all, i)
    return xf.reshape(batch, s, d)
```

```python
import functools
import math

import numpy as np
import jax
import jax.numpy as jnp
from jax import lax
from jax.experimental import pallas as pl
from jax.experimental.pallas import tpu as pltpu

F32 = jnp.float32
BF16 = jnp.bfloat16
MIX_DTYPE = BF16
ACT_DTYPE = BF16

NORM_EPS = 1e-6
LANES = 128
BF16_ROWS = 16
VMEM_LIMIT_BYTES = 56 * 1024 * 1024

D_MODEL = 2048
PLE_DIM = 256
POOL_WINDOWS = (2, 4, 8, 16)
POOL_GROUP = 128
POOL_WIDTH = 512
POOL_HALO = 16
GLA_HEADS = 4
GLA_DK = 192
GLA_DV = 384
GLA_QK = GLA_HEADS * GLA_DK
GLA_WIDTH = GLA_HEADS * GLA_DV
GLA_GATE_RANK = 16
GLA_TAU = 16.0
RWKV_HEAD = 64
RWKV_WIDTH = 1024
RWKV_DECAY_RANK = 64
RWKV_A_RANK = 64
RWKV_GATE_RANK = 160
RWKV_LR = RWKV_DECAY_RANK + RWKV_A_RANK + RWKV_GATE_RANK
RWKV_LR_PAD = 384
RWKV_LN_EPS = 64e-5
DIL_WIDTH = 1024
DIL_HEAD = 128
DIL_HEADS = 8
DIL_PATTERNS = ((128, 1), (512, 4), (2048, 16))
DIL_BLOCK = 128
ROPE_THETA = 10000.0

ROW_CHUNK = 256
PROJ_TM = 1024
MIX_TM = 1024
FFN_TM = 512
FFN_TF = 1024
PLE_TM = 1024
POOL_TS = 512
GLA_CHUNK = 64
GLA_TS = 1024
GLA_GROUP = 8
RWKV_CHUNK = 64
RWKV_TS = 1024
RWKV_GROUP = 16
DIL_TS = 2048


def _params(*sem):
    return pltpu.CompilerParams(dimension_semantics=sem, vmem_limit_bytes=VMEM_LIMIT_BYTES)


def _dot(a, b):
    return jnp.dot(a.astype(BF16), b.astype(BF16), preferred_element_type=F32)


def _dot_nt(a, b):
    return lax.dot_general(a.astype(BF16), b.astype(BF16), (((1,), (1,)), ((), ())),
                           preferred_element_type=F32)


def _dot_tn(a, b):
    return lax.dot_general(a.astype(BF16), b.astype(BF16), (((0,), (0,)), ((), ())),
                           preferred_element_type=F32)


def _split(x):
    hi = x.astype(BF16)
    return hi, (x - hi.astype(F32)).astype(BF16)


def _dot3_general(a, b, dims):
    ah, al = _split(a)
    bh, bl = _split(b)
    d = lambda u, v: lax.dot_general(u, v, (dims, ((), ())), preferred_element_type=F32)
    return d(ah, bh) + d(ah, bl) + d(al, bh)


def _dot3(a, b):
    return _dot3_general(a, b, ((1,), (0,)))


def _dot3_nt(a, b):
    return _dot3_general(a, b, ((1,), (1,)))


def _dot_sel(sel_bf16, x):
    hi = x.astype(BF16)
    r1 = x - hi.astype(F32)
    mid = r1.astype(BF16)
    lo = (r1 - mid.astype(F32)).astype(BF16)
    out = jnp.dot(sel_bf16, hi, preferred_element_type=F32)
    out = out + jnp.dot(sel_bf16, mid, preferred_element_type=F32)
    return out + jnp.dot(sel_bf16, lo, preferred_element_type=F32)


def _row_chunks(rows):
    step = min(ROW_CHUNK, rows)
    return [pl.ds(r, step) for r in range(0, rows, step)]


def _rms(x, gain):
    return x * lax.rsqrt(jnp.mean(x * x, axis=-1, keepdims=True) + NORM_EPS) * gain


def _sigmoid(x):
    return 1.0 / (1.0 + jnp.exp(-x))


def _log_sigmoid(x):
    return jnp.minimum(x, 0.0) - jnp.log1p(jnp.exp(-jnp.abs(x)))


def _proj_kernel(starts, x_ref, gain_ref, *refs):
    nw = len(starts) - 1
    w_refs, o_refs, h_ref = refs[:nw], refs[nw:2 * nw], refs[2 * nw]
    j = pl.program_id(1)

    @pl.when(j == 0)
    def _():
        for rows in _row_chunks(x_ref.shape[0]):
            h = _rms(x_ref[rows, :], gain_ref[...]).astype(BF16)
            h_ref[rows, :] = h
            o_refs[0][rows, :] = jnp.dot(h, w_refs[0][...], preferred_element_type=F32).astype(o_refs[0].dtype)

    for k in range(nw):
        @pl.when((j >= max(starts[k], 1)) & (j < starts[k + 1]))
        def _(k=k):
            o_refs[k][...] = jnp.dot(h_ref[...], w_refs[k][...],
                                     preferred_element_type=F32).astype(o_refs[k].dtype)


def _proj(x, gain, ws, dtypes):
    m, d = x.shape
    tm = min(PROJ_TM, m)
    starts = [0]
    for w in ws:
        starts.append(starts[-1] + w.shape[0])

    def group(k):
        return lambda i, j: jnp.clip(j - starts[k], 0, ws[k].shape[0] - 1)

    w_specs = [pl.BlockSpec((None, d, w.shape[2]), lambda i, j, gk=group(k): (gk(i, j), 0, 0))
               for k, w in enumerate(ws)]
    o_specs = [pl.BlockSpec((None, tm, w.shape[2]), lambda i, j, gk=group(k): (gk(i, j), i, 0))
               for k, w in enumerate(ws)]
    return pl.pallas_call(
        functools.partial(_proj_kernel, tuple(starts)),
        grid=(m // tm, starts[-1]),
        in_specs=[pl.BlockSpec((tm, d), lambda i, j: (i, 0)), pl.BlockSpec((1, d), lambda i, j: (0, 0))] + w_specs,
        out_specs=o_specs,
        out_shape=[jax.ShapeDtypeStruct((w.shape[0], m, w.shape[2]), dt) for w, dt in zip(ws, dtypes)],
        scratch_shapes=[pltpu.VMEM((tm, d), BF16)],
        compiler_params=_params("parallel", "arbitrary"),
        name="norm_proj",
    )(x, gain, *ws)


def _mix_out_kernel(a1_ref, a2_ref, w1_ref, w2_ref, x_ref, gain_ref, o_ref):
    for rows in _row_chunks(x_ref.shape[0]):
        y = _dot(a1_ref[rows, :], w1_ref[...]) + _dot(a2_ref[rows, :], w2_ref[...])
        o_ref[rows, :] = x_ref[rows, :] + _rms(y, gain_ref[...])


def _resident(shape, index_map):
    return pl.BlockSpec(shape, index_map, pipeline_mode=pl.Buffered(1))


def _mix_out(a1, a2, w1, w2, x, gain):
    m, d = x.shape
    k1, k2 = a1.shape[1], a2.shape[1]
    tm = min(MIX_TM, m)
    return pl.pallas_call(
        _mix_out_kernel,
        grid=(m // tm,),
        in_specs=[
            pl.BlockSpec((tm, k1), lambda i: (i, 0)),
            pl.BlockSpec((tm, k2), lambda i: (i, 0)),
            _resident((k1, d), lambda i: (0, 0)),
            _resident((k2, d), lambda i: (0, 0)),
            pl.BlockSpec((tm, d), lambda i: (i, 0)),
            pl.BlockSpec((1, d), lambda i: (0, 0)),
        ],
        out_specs=pl.BlockSpec((tm, d), lambda i: (i, 0)),
        out_shape=jax.ShapeDtypeStruct((m, d), F32),
        compiler_params=_params("parallel"),
        name="mix_out",
    )(a1, a2, w1, w2, x, gain)


def _ffn_kernel(x_ref, gpre_ref, gpost_ref, up_ref, down_ref, o_ref, h_ref, acc_ref):
    f = pl.program_id(1)
    last = pl.num_programs(1) - 1

    def contribution(h):
        a = jnp.maximum(jnp.dot(h, up_ref[...], preferred_element_type=F32), 0.0)
        return jnp.dot((a * a).astype(BF16), down_ref[...], preferred_element_type=F32)

    @pl.when(f == 0)
    def _():
        for rows in _row_chunks(x_ref.shape[0]):
            h = _rms(x_ref[rows, :], gpre_ref[...]).astype(BF16)
            h_ref[rows, :] = h
            acc_ref[rows, :] = contribution(h)

    @pl.when((f > 0) & (f < last))
    def _():
        acc_ref[...] += contribution(h_ref[...])

    @pl.when(f == last)
    def _():
        for rows in _row_chunks(x_ref.shape[0]):
            y = acc_ref[rows, :] + contribution(h_ref[rows, :])
            o_ref[rows, :] = x_ref[rows, :] + _rms(y, gpost_ref[...])


def _ffn(x, gpre, gpost, up, down, layer):
    m, d = x.shape
    dff = up.shape[2]
    tm = min(FFN_TM, m)
    tf = min(FFN_TF, dff)
    assert dff // tf >= 2
    return pl.pallas_call(
        _ffn_kernel,
        grid=(m // tm, dff // tf),
        in_specs=[
            pl.BlockSpec((tm, d), lambda i, f: (i, 0)),
            pl.BlockSpec((1, d), lambda i, f: (0, 0)),
            pl.BlockSpec((1, d), lambda i, f: (0, 0)),
            pl.BlockSpec((None, d, tf), lambda i, f: (layer, 0, f)),
            pl.BlockSpec((None, tf, d), lambda i, f: (layer, f, 0)),
        ],
        out_specs=pl.BlockSpec((tm, d), lambda i, f: (i, 0)),
        out_shape=jax.ShapeDtypeStruct((m, d), F32),
        scratch_shapes=[pltpu.VMEM((tm, d), BF16), pltpu.VMEM((tm, d), F32)],
        compiler_params=_params("parallel", "arbitrary"),
        name="ffn",
    )(x, gpre, gpost, up, down)


def _ple_kernel(x_ref, gain_ref, wg_ref, p_ref, wp_ref, o_ref):
    for rows in _row_chunks(x_ref.shape[0]):
        x = x_ref[rows, :]
        gate = _sigmoid(_dot(_rms(x, gain_ref[...]), wg_ref[...]))
        o_ref[rows, :] = x + _dot(p_ref[rows, :], wp_ref[...]) * gate


def _ple(x, gain, wg, p, wp, layer):
    m, d = x.shape
    pd = p.shape[2]
    tm = min(PLE_TM, m)
    return pl.pallas_call(
        _ple_kernel,
        grid=(m // tm,),
        in_specs=[
            pl.BlockSpec((tm, d), lambda i: (i, 0)),
            pl.BlockSpec((1, d), lambda i: (0, 0)),
            _resident((None, d, d), lambda i: (layer, 0, 0)),
            pl.BlockSpec((None, tm, pd), lambda i: (layer, i, 0)),
            _resident((None, pd, d), lambda i: (layer, 0, 0)),
        ],
        out_specs=pl.BlockSpec((tm, d), lambda i: (i, 0)),
        out_shape=jax.ShapeDtypeStruct((m, d), F32),
        compiler_params=_params("parallel"),
        name="ple",
    )(x, gain, wg, p, wp)


def _pool_kernel(u_ref, prev_ref, w_ref, scale_ref, o_ref, ext_ref):
    i = pl.program_id(1)
    ts = u_ref.shape[0]
    ext_ref[pl.ds(POOL_HALO, ts), :] = u_ref[...]

    @pl.when(i == 0)
    def _():
        ext_ref[pl.ds(0, POOL_HALO), :] = jnp.zeros((POOL_HALO, POOL_WIDTH), F32)

    @pl.when(i > 0)
    def _():
        ext_ref[pl.ds(0, POOL_HALO), :] = prev_ref[...]

    t = i * ts + lax.broadcasted_iota(jnp.int32, (ts, 1), 0)
    for gi, w in enumerate(POOL_WINDOWS):
        lo = gi * POOL_GROUP
        cur = ext_ref[pl.ds(POOL_HALO, ts), pl.ds(lo, POOL_GROUP)]
        tot = cur
        for j in range(1, w):
            tot = tot + ext_ref[pl.ds(POOL_HALO - j, ts), pl.ds(lo, POOL_GROUP)]
        cnt = jnp.minimum(t + 1, w).astype(F32)
        pooled = tot / cnt - cur
        out = _dot(pooled, w_ref[gi]) * scale_ref[:, pl.ds(lo, POOL_GROUP)]
        o_ref[:, pl.ds(lo, POOL_GROUP)] = out.astype(o_ref.dtype)


def _pool(u, pool_w, pool_scale, batch):
    m, width = u.shape
    s = m // batch
    ts = min(POOL_TS, s)
    nt = s // ts
    hb = ts // POOL_HALO
    return pl.pallas_call(
        _pool_kernel,
        grid=(batch, nt),
        in_specs=[
            pl.BlockSpec((ts, width), lambda b, i: (b * nt + i, 0)),
            pl.BlockSpec((POOL_HALO, width), lambda b, i: (jnp.maximum((b * nt + i) * hb - 1, 0), 0)),
            pl.BlockSpec((len(POOL_WINDOWS), POOL_GROUP, POOL_GROUP), lambda b, i: (0, 0, 0)),
            pl.BlockSpec((1, width), lambda b, i: (0, 0)),
        ],
        out_specs=pl.BlockSpec((ts, width), lambda b, i: (b * nt + i, 0)),
        out_shape=jax.ShapeDtypeStruct((m, width), MIX_DTYPE),
        scratch_shapes=[pltpu.VMEM((POOL_HALO + ts, width), F32)],
        compiler_params=_params("parallel", "arbitrary"),
        name="pool_mixer",
    )(u, u, pool_w, pool_scale)


def _gla_constants(c):
    nlev = int(math.log2(c))
    i = np.arange(c)[:, None]
    j = np.arange(c)[None, :]
    x = i ^ j
    level = np.full((c, c), -1, np.int32)
    for lev in range(nlev):
        level = np.where((x >> lev) == 1, lev, level)
    level = np.where(j > i, -1, level)
    level = np.where(i == j, nlev, level)
    return jnp.asarray(j <= i, BF16), jnp.asarray(level, jnp.int32)


def _gla_kernel(z_ref, glr_ref, w2_ref, gb_ref, gn_ref, tri_ref, lev_ref, o_ref, st_ref, cum_ref):
    c = GLA_CHUNK
    nlev = int(math.log2(c))
    ts = z_ref.shape[0]
    sub = 8
    kq, kk, kv, kg = 0, GLA_DK, 2 * GLA_DK, 2 * GLA_DK + GLA_DV

    @pl.when(pl.program_id(2) == 0)
    def _():
        st_ref[...] = jnp.zeros_like(st_ref)

    tri = tri_ref[...]
    level = lev_ref[...]
    gn = gn_ref[...]
    row8 = lax.broadcasted_iota(jnp.int32, (sub, GLA_DK), 0)

    g_all = _log_sigmoid(_dot3(glr_ref[...], w2_ref[...]) + gb_ref[...]) / GLA_TAU
    for ci in range(ts // c):
        cum_ref[pl.ds(ci * c, c), :] = _dot_sel(tri, g_all[ci * c:(ci + 1) * c])

    def cum_row(r):
        return jnp.broadcast_to(cum_ref[pl.ds(r, 1), :], (sub, GLA_DK))

    def level_exponent(ci, lev):
        s = 1 << lev
        tiles = []
        for a in range(c // sub):
            r0 = ci * c + a * sub
            b = cum_ref[pl.ds(r0, sub), :]
            if 2 * s >= sub:
                bm = cum_row(ci * c + (a * sub // (2 * s)) * (2 * s) + s - 1)
            else:
                bm = cum_row(r0 + s - 1)
                for blk in range(2 * s, sub, 2 * s):
                    bm = jnp.where(row8 >= blk, cum_row(r0 + blk + s - 1), bm)
            tiles.append(-jnp.abs(b - bm))
        return jnp.concatenate(tiles, axis=0)

    st = st_ref[...]
    for gi in range(ts // (c * GLA_GROUP)):
        ids = range(gi * GLA_GROUP, (gi + 1) * GLA_GROUP)
        rows = {ci: pl.ds(ci * c, c) for ci in ids}
        q = {ci: z_ref[rows[ci], kq:kq + GLA_DK].astype(F32) * (GLA_DK ** -0.5) for ci in ids}
        k = {ci: z_ref[rows[ci], kk:kk + GLA_DK].astype(F32) for ci in ids}
        b = {ci: cum_ref[rows[ci], :] for ci in ids}
        b_last = {ci: cum_ref[pl.ds(ci * c + c - 1, 1), :] for ci in ids}
        att = {ci: jnp.where(level == nlev, _dot_nt(q[ci], k[ci]), 0.0) for ci in ids}
        for lev in range(nlev):
            el = {ci: jnp.exp(level_exponent(ci, lev)) for ci in ids}
            att = {ci: jnp.where(level == lev, _dot_nt(q[ci] * el[ci], k[ci] * el[ci]), att[ci]) for ci in ids}
        intra = {ci: _dot(att[ci], z_ref[rows[ci], kv:kv + GLA_DV]) for ci in ids}
        qe = {ci: q[ci] * jnp.exp(b[ci]) for ci in ids}
        ke = {ci: k[ci] * jnp.exp(b_last[ci] - b[ci]) for ci in ids}
        for ci in ids:
            o = _dot_nt(qe[ci], st) + intra[ci]
            st = st * jnp.exp(b_last[ci]) + _dot_tn(z_ref[rows[ci], kv:kv + GLA_DV], ke[ci])
            o = o * lax.rsqrt(jnp.mean(o * o, axis=-1, keepdims=True) + NORM_EPS) * gn
            gate = z_ref[rows[ci], kg:kg + GLA_DV].astype(F32)
            o_ref[rows[ci], :] = (o * (gate * _sigmoid(gate))).astype(o_ref.dtype)
    st_ref[...] = st


def _gla(z, glr, w2p, gate_b, gla_norm, batch):
    h, m, zw = z.shape
    s = m // batch
    ts = min(GLA_TS, s)
    nt = s // ts
    tri, level = _gla_constants(GLA_CHUNK)
    row = lambda b, hh, i: b * nt + i
    return pl.pallas_call(
        _gla_kernel,
        grid=(batch, h, nt),
        in_specs=[
            pl.BlockSpec((None, ts, zw), lambda b, hh, i: (hh, row(b, hh, i), 0)),
            pl.BlockSpec((ts, LANES), lambda b, hh, i: (row(b, hh, i), 0)),
            pl.BlockSpec((None, LANES, GLA_DK), lambda b, hh, i: (hh, 0, 0)),
            pl.BlockSpec((None, 1, GLA_DK), lambda b, hh, i: (hh, 0, 0)),
            pl.BlockSpec((1, GLA_DV), lambda b, hh, i: (0, 0)),
            pl.BlockSpec(tri.shape, lambda b, hh, i: (0, 0)),
            pl.BlockSpec(level.shape, lambda b, hh, i: (0, 0)),
        ],
        out_specs=pl.BlockSpec((ts, GLA_DV), lambda b, hh, i: (row(b, hh, i), hh)),
        out_shape=jax.ShapeDtypeStruct((m, GLA_WIDTH), MIX_DTYPE),
        scratch_shapes=[pltpu.VMEM((GLA_DV, GLA_DK), F32), pltpu.VMEM((ts, GLA_DK), F32)],
        compiler_params=_params("parallel", "parallel", "arbitrary"),
        name="gla",
    )(z, glr, w2p, gate_b, gla_norm, tri, level)


def _head_sum_matrix():
    lane = np.arange(LANES)
    return jnp.asarray((lane[:, None] // RWKV_HEAD) == (lane[None, :] // RWKV_HEAD), BF16)


def _shifted(cur, prev_rows, first):
    last = prev_rows.shape[0] - 1
    prev_row = jnp.where(first, 0.0, prev_rows[last:last + 1, :])
    rolled = pltpu.roll(cur, 1, 0)
    row = lax.broadcasted_iota(jnp.int32, cur.shape, 0)
    return jnp.where(row == 0, prev_row, rolled)


def _rwkv_prepare(first, rkv_ref, rkvp_ref, lr_ref, lrp_ref, mu_ref, mulr_ref, w0_ref, w2_ref, a0_ref,
                  a2_ref, g2_ref, kk_ref, ka_ref, hs_ref,
                  r_out, lw_out, k_out, v_out, a_out, b_out, g_out):
    def mixed(cur, prev_rows, mu):
        cur, prev_rows = cur.astype(F32), prev_rows.astype(F32)
        return cur + (_shifted(cur, prev_rows, first) - cur) * mu

    r = mixed(rkv_ref[0], rkvp_ref[0], mu_ref[0])
    k = mixed(rkv_ref[1], rkvp_ref[1], mu_ref[1])
    v = mixed(rkv_ref[2], rkvp_ref[2], mu_ref[2])
    lr = mixed(lr_ref[...], lrp_ref[...], mulr_ref[...])

    lr_wa = lr[:, 0:LANES]
    lr_g = lr[:, LANES:RWKV_LR_PAD]
    w_log = _log_sigmoid(w0_ref[...] + _dot(jnp.tanh(lr_wa), w2_ref[...])) - 0.5
    lw = -jnp.exp(w_log)
    a = _sigmoid(a0_ref[...] + _dot(lr_wa, a2_ref[...]))
    g = _dot(_sigmoid(lr_g), g2_ref[...])

    kk = k * kk_ref[...]
    norm = jnp.sqrt(_dot(kk * kk, hs_ref[...]))
    kk = kk / jnp.maximum(norm, 1e-12)
    r_out[...] = r
    lw_out[...] = lw
    k_out[...] = k * (1.0 + (a - 1.0) * ka_ref[...])
    v_out[...] = v
    a_out[...] = -kk
    b_out[...] = kk * a
    g_out[...] = g


def _rwkv_constants(c):
    i = np.arange(c)[:, None]
    j = np.arange(c)[None, :]
    tri = jnp.asarray(j <= i, BF16)
    i2 = np.arange(2 * c)[:, None] % c
    j2 = np.arange(2 * c)[None, :] % c
    strict = jnp.asarray(i2 > j2, F32)
    incl = jnp.asarray(i2 >= j2, F32)
    return tri, strict, incl


def _rwkv_kernel(rkv_ref, rkvp_ref, lr_ref, lrp_ref, mu_ref, mulr_ref, w0_ref, w2_ref, a0_ref, a2_ref, g2_ref,
                 kk_ref, ka_ref, lnw_ref, lnb_ref, rk_ref, tri_ref, strict_ref, incl_ref, hs_ref, o_ref,
                 state_ref, p_ref, gm_ref, rq_ref, y0_ref, r_ref, lw_ref, k_ref, v_ref, a_ref, b_ref, g_ref, y_ref):
    c = RWKV_CHUNK
    nlev = int(math.log2(c))
    ts = o_ref.shape[0]
    nchunk = ts // c
    first = pl.program_id(2) == 0

    @pl.when(first)
    def _():
        state_ref[...] = jnp.zeros_like(state_ref)

    _rwkv_prepare(first, rkv_ref, rkvp_ref, lr_ref, lrp_ref, mu_ref, mulr_ref, w0_ref, w2_ref, a0_ref, a2_ref,
                  g2_ref, kk_ref, ka_ref, hs_ref, r_ref, lw_ref, k_ref, v_ref, a_ref, b_ref, g_ref)

    lane = lax.broadcasted_iota(jnp.int32, (1, LANES), 1)
    head0 = lane < RWKV_HEAD
    tri = tri_ref[...]
    strict = strict_ref[...]
    incl = incl_ref[...]
    eye = (lax.broadcasted_iota(jnp.int32, (LANES, LANES), 0)
           == lax.broadcasted_iota(jnp.int32, (LANES, LANES), 1)).astype(F32)

    def stack(x):
        return jnp.concatenate([jnp.where(head0, x, 0.0), jnp.where(head0, 0.0, x)], axis=0)

    grp = RWKV_GROUP
    each = range(grp)

    def local(gi):
        base = gi * grp
        rows = [pl.ds((base + t) * c, c) for t in each]
        lw = [lw_ref[rows[t], :] for t in each]
        cum = [_dot_sel(tri, lw[t]) for t in each]
        last = [cum[t][c - 1:c] for t in each]
        grow = [jnp.exp(-cum[t]) for t in each]
        fall = [jnp.exp(last[t] - cum[t]) for t in each]
        at2 = [stack(a_ref[rows[t], :] * jnp.exp(cum[t] - lw[t])) for t in each]
        rt2 = [stack(r_ref[rows[t], :] * jnp.exp(cum[t])) for t in each]
        bt2 = [stack(b_ref[rows[t], :] * grow[t]) for t in each]
        kt2 = [stack(k_ref[rows[t], :] * grow[t]) for t in each]
        bh2 = [stack(b_ref[rows[t], :] * fall[t]) for t in each]
        kh2 = [stack(k_ref[rows[t], :] * fall[t]) for t in each]
        v2 = [stack(v_ref[rows[t], :]) for t in each]
        big = [_dot_nt(jnp.concatenate([at2[t], rt2[t]], axis=0), jnp.concatenate([bt2[t], kt2[t]], axis=0))
               for t in each]
        a_ab = [big[t][0:2 * c, 0:2 * c] * strict for t in each]
        a_ak = [big[t][0:2 * c, 2 * c:4 * c] * strict for t in each]
        a_rb = [big[t][2 * c:4 * c, 0:2 * c] * incl for t in each]
        a_rk = [big[t][2 * c:4 * c, 2 * c:4 * c] * incl for t in each]
        rhs = [jnp.concatenate([at2[t], _dot(a_ak[t], v2[t])], axis=1) for t in each]
        n = a_ab
        for lev in range(nlev):
            rhs = [rhs[t] + _dot(n[t], rhs[t]) for t in each]
            if lev + 1 < nlev:
                n = [_dot(n[t], n[t]) for t in each]
        for t in each:
            w2 = rhs[t][:, 0:LANES]
            z2 = rhs[t][:, LANES:2 * LANES]
            gamma = jnp.exp(last[t])
            p_ref[base + t] = eye * gamma + _dot_tn(w2, bh2[t])
            gm_ref[base + t] = _dot_tn(jnp.concatenate([z2, v2[t]], axis=0),
                                       jnp.concatenate([bh2[t], kh2[t]], axis=0))
            rq_ref[base + t] = rt2[t] + _dot(a_rb[t], w2)
            y0_ref[base + t] = _dot(a_rb[t], z2) + _dot(a_rk[t], v2[t])

    st = state_ref[...]
    for gi in range(nchunk // grp):
        local(gi)
        for ci in range(gi * grp, (gi + 1) * grp):
            y2 = _dot_nt(rq_ref[ci], st) + y0_ref[ci]
            y_ref[pl.ds(ci * c, c), :] = y2[0:c] + y2[c:2 * c]
            st = _dot3(st, p_ref[ci]) + gm_ref[ci]
    state_ref[...] = st

    hs = hs_ref[...]
    y = y_ref[...]
    mean = _dot(y, hs) * (1.0 / RWKV_HEAD)
    yc = y - mean
    var = _dot(yc * yc, hs) * (1.0 / RWKV_HEAD)
    yn = yc * lax.rsqrt(var + RWKV_LN_EPS) * lnw_ref[...] + lnb_ref[...]
    bonus = _dot(r_ref[...] * k_ref[...] * rk_ref[...], hs) * v_ref[...]
    o_ref[...] = ((yn + bonus) * g_ref[...]).astype(o_ref.dtype)


def _rwkv(z6, lr, mu_rkv, mu_lr, w0, w2p, a0, a2p, g2p, k_k, k_a, ln_w, ln_b, r_k, batch):
    m = lr.shape[0]
    s = m // batch
    ts = min(RWKV_TS, s)
    nt = s // ts
    nb = RWKV_WIDTH // LANES
    c = RWKV_CHUNK
    nchunk = ts // c
    tri, strict, incl = _rwkv_constants(c)
    hs = _head_sum_matrix()
    row = lambda bb, j, i: bb * nt + i
    prow = lambda bb, j, i: jnp.maximum((bb * nt + i) * (ts // 8) - 1, 0)
    vecspec = pl.BlockSpec((1, LANES), lambda bb, j, i: (0, j))
    lrw = pl.BlockSpec((LANES, LANES), lambda bb, j, i: (0, j))
    lrg = pl.BlockSpec((RWKV_LR_PAD - LANES, LANES), lambda bb, j, i: (0, j))
    const = lambda arr: pl.BlockSpec(arr.shape, lambda bb, j, i: (0,) * arr.ndim)
    tile = pltpu.VMEM((ts, LANES), F32)
    return pl.pallas_call(
        _rwkv_kernel,
        grid=(batch, nb, nt),
        in_specs=[
            pl.BlockSpec((3, ts, LANES), lambda bb, j, i: (0, row(bb, j, i), j)),
            pl.BlockSpec((3, BF16_ROWS, LANES),
                         lambda bb, j, i: (0, jnp.maximum((bb * nt + i) * (ts // BF16_ROWS) - 1, 0), j)),
            pl.BlockSpec((ts, RWKV_LR_PAD), lambda bb, j, i: (row(bb, j, i), 0)),
            pl.BlockSpec((8, RWKV_LR_PAD), lambda bb, j, i: (prow(bb, j, i), 0)),
            pl.BlockSpec((3, 1, LANES), lambda bb, j, i: (0, 0, j)),
            pl.BlockSpec((1, RWKV_LR_PAD), lambda bb, j, i: (0, 0)),
            vecspec, lrw, vecspec, lrw, lrg, vecspec, vecspec,
            vecspec, vecspec, vecspec,
            const(tri), const(strict), const(incl), const(hs),
        ],
        out_specs=pl.BlockSpec((ts, LANES), lambda bb, j, i: (row(bb, j, i), j)),
        out_shape=jax.ShapeDtypeStruct((m, RWKV_WIDTH), MIX_DTYPE),
        scratch_shapes=[
            pltpu.VMEM((LANES, LANES), F32),
            pltpu.VMEM((nchunk, LANES, LANES), F32),
            pltpu.VMEM((nchunk, LANES, LANES), F32),
            pltpu.VMEM((nchunk, 2 * c, LANES), F32),
            pltpu.VMEM((nchunk, 2 * c, LANES), F32),
        ] + [tile] * 8,
        compiler_params=_params("parallel", "parallel", "arbitrary"),
        name="rwkv",
    )(z6, z6, lr, lr, mu_rkv, mu_lr, w0, w2p, a0, a2p, g2p, k_k, k_a, ln_w, ln_b, r_k, tri, strict, incl, hs)


def _rope(x, cos2, sin2):
    return x * cos2 + pltpu.roll(x, DIL_HEAD // 2, 1) * sin2


def _dil_kernel(q_ref, kp_ref, kc_ref, vp_ref, vc_ref, cos_ref, sin_ref, cosp_ref, sinp_ref, o_ref,
                qs_ref, ks_ref, vs_ref, acc_ref, m_ref, l_ref):
    ts = q_ref.shape[0]
    i = pl.program_id(2)
    blk = DIL_BLOCK
    qs_ref[...] = _rope(q_ref[...].astype(F32), cos_ref[...], sin_ref[...]) * (DIL_HEAD ** -0.5)
    ks_ref[pl.ds(0, ts), :] = _rope(kp_ref[...].astype(F32), cosp_ref[...], sinp_ref[...])
    ks_ref[pl.ds(ts, ts), :] = _rope(kc_ref[...].astype(F32), cos_ref[...], sin_ref[...])
    vs_ref[pl.ds(0, ts), :] = vp_ref[...].astype(F32)
    vs_ref[pl.ds(ts, ts), :] = vc_ref[...].astype(F32)

    qi = lax.broadcasted_iota(jnp.int32, (blk, 2 * blk), 0)
    ki = lax.broadcasted_iota(jnp.int32, (blk, 2 * blk), 1)
    dist = qi + blk - ki
    band = (dist >= 0) & (dist <= blk)

    for pi, (window, dil) in enumerate(DIL_PATTERNS):
        assert window // dil == blk
        per_res = ts // (blk * dil)

        def unit(u, carry, pi=pi, dil=dil, per_res=per_res):
            res = u // per_res
            nblk = u % per_res
            qstart = res + nblk * (blk * dil)
            kstart = ts + qstart - blk * dil
            q = qs_ref[pl.ds(qstart, blk, stride=dil), :]
            k = ks_ref[pl.ds(kstart, 2 * blk, stride=dil), :]
            v = vs_ref[pl.ds(kstart, 2 * blk, stride=dil), :]
            s = _dot_nt(q, k)
            key_pos = i * ts + (qstart - blk * dil) + ki * dil
            s = jnp.where(band & (key_pos >= 0), s, -jnp.inf)
            mx = jnp.max(s, axis=-1, keepdims=True)
            p = jnp.exp(s - mx)
            acc_ref[pi, pl.ds(qstart, blk, stride=dil), :] = _dot(p, v)
            m_ref[pi, pl.ds(qstart, blk, stride=dil), :] = jnp.broadcast_to(mx, (blk, LANES))
            l_ref[pi, pl.ds(qstart, blk, stride=dil), :] = jnp.broadcast_to(
                jnp.sum(p, axis=-1, keepdims=True), (blk, LANES))
            return carry

        lax.fori_loop(0, ts // blk, unit, 0, unroll=8)

    m_all = jnp.maximum(jnp.maximum(m_ref[0], m_ref[1]), m_ref[2])
    num = jnp.zeros((ts, LANES), F32)
    den = jnp.zeros((ts, LANES), F32)
    for pi in range(len(DIL_PATTERNS)):
        cf = jnp.exp(m_ref[pi] - m_all)
        num = num + cf * acc_ref[pi]
        den = den + cf * l_ref[pi]
    o_ref[...] = (num / den).astype(o_ref.dtype)


def _dilated(z6, cos2, sin2, batch):
    m = z6.shape[1]
    s = m // batch
    ts = DIL_TS
    assert s % ts == 0
    nt = s // ts
    cur = lambda which: pl.BlockSpec((None, ts, DIL_HEAD), lambda b, h, i: (which, b * nt + i, h))
    prev = lambda which: pl.BlockSpec((None, ts, DIL_HEAD),
                                      lambda b, h, i: (which, b * nt + jnp.maximum(i - 1, 0), h))
    tab = pl.BlockSpec((ts, DIL_HEAD), lambda b, h, i: (i, 0))
    tabp = pl.BlockSpec((ts, DIL_HEAD), lambda b, h, i: (jnp.maximum(i - 1, 0), 0))
    npat = len(DIL_PATTERNS)
    return pl.pallas_call(
        _dil_kernel,
        grid=(batch, DIL_HEADS, nt),
        in_specs=[cur(3), prev(4), cur(4), prev(5), cur(5), tab, tab, tabp, tabp],
        out_specs=pl.BlockSpec((ts, DIL_HEAD), lambda b, h, i: (b * nt + i, h)),
        out_shape=jax.ShapeDtypeStruct((m, DIL_WIDTH), MIX_DTYPE),
        scratch_shapes=[
            pltpu.VMEM((ts, DIL_HEAD), F32),
            pltpu.VMEM((2 * ts, DIL_HEAD), F32),
            pltpu.VMEM((2 * ts, DIL_HEAD), F32),
            pltpu.VMEM((npat, ts, DIL_HEAD), F32),
            pltpu.VMEM((npat, ts, LANES), F32),
            pltpu.VMEM((npat, ts, LANES), F32),
        ],
        compiler_params=_params("parallel", "parallel", "arbitrary"),
        name="dilated_attention",
    )(z6, z6, z6, z6, z6, cos2, sin2, cos2, sin2)


def _heads(w, n_heads, width):
    d = w.shape[0]
    return w.reshape(d, n_heads, width).transpose(1, 0, 2)


def _pad_rows(w, start, total):
    return jnp.zeros((total, w.shape[1]), w.dtype).at[start:start + w.shape[0]].set(w)


def _even_mixer(x, gain_pre, gain_post, w_in, w_out, pool_w, pool_scale, gate_w2, gate_b, gla_norm, batch):
    d = x.shape[1]
    o0 = POOL_WIDTH
    o1 = o0 + GLA_QK
    o2 = o1 + GLA_QK
    o3 = o2 + GLA_WIDTH
    o4 = o3 + GLA_WIDTH
    wb = w_in.astype(BF16)
    w_u = wb[:, :o0][None]
    w_z = jnp.concatenate([_heads(wb[:, o0:o1], GLA_HEADS, GLA_DK), _heads(wb[:, o1:o2], GLA_HEADS, GLA_DK),
                           _heads(wb[:, o2:o3], GLA_HEADS, GLA_DV), _heads(wb[:, o3:o4], GLA_HEADS, GLA_DV)], 2)
    w_glr = jnp.zeros((d, LANES), BF16).at[:, :GLA_GATE_RANK].set(wb[:, o4:])[None]
    u, glr, z = _proj(x, gain_pre, [w_u, w_glr, w_z], [F32, F32, ACT_DTYPE])
    u, glr = u[0], glr[0]
    a_out = _pool(u, pool_w.astype(BF16), pool_scale[None], batch)
    w2p = _heads(_pad_rows(gate_w2, 0, LANES), GLA_HEADS, GLA_DK)
    o = _gla(z, glr, w2p, gate_b.reshape(GLA_HEADS, 1, GLA_DK), gla_norm[None], batch)
    wo = w_out.astype(BF16)
    return _mix_out(a_out, o, wo[:POOL_WIDTH], wo[POOL_WIDTH:], x, gain_post)


def _odd_mixer(x, gain_pre, gain_post, w_in, w_out, mu, w0, w2, a0, a2, g2, k_k, k_a, r_k, ln_w, ln_b,
               cos2, sin2, batch):
    d = x.shape[1]
    wb = w_in.astype(BF16)
    c3 = 3 * RWKV_WIDTH
    rwkv_in = c3 + RWKV_LR
    w6 = jnp.concatenate([_heads(wb[:, :c3], 3, RWKV_WIDTH), _heads(wb[:, rwkv_in:], 3, DIL_WIDTH)], 0)
    w_lr = jnp.zeros((d, RWKV_LR_PAD), BF16).at[:, :RWKV_LR].set(wb[:, c3:rwkv_in])[None]
    z6, lr = _proj(x, gain_pre, [w6, w_lr], [ACT_DTYPE, F32])
    lr = lr[0]
    mu_rkv = mu[:c3].reshape(3, 1, RWKV_WIDTH)
    mu_lr = jnp.zeros((1, RWKV_LR_PAD), F32).at[0, :RWKV_LR].set(mu[c3:])
    w2p = _pad_rows(w2, 0, LANES).astype(BF16)
    a2p = _pad_rows(a2, RWKV_DECAY_RANK, LANES).astype(BF16)
    g2p = _pad_rows(g2, 0, RWKV_LR_PAD - LANES).astype(BF16)
    c_out = _rwkv(z6, lr, mu_rkv, mu_lr, w0[None], w2p, a0[None], a2p, g2p, k_k[None], k_a[None],
                  ln_w[None], ln_b[None], r_k.reshape(1, RWKV_WIDTH), batch)
    d_out = _dilated(z6, cos2, sin2, batch)
    wo = w_out.astype(BF16)
    return _mix_out(c_out, d_out, wo[:RWKV_WIDTH], wo[RWKV_WIDTH:], x, gain_post)


def _rope_tables(s):
    half = DIL_HEAD // 2
    inv = ROPE_THETA ** (-jnp.arange(half, dtype=F32) / half)
    ang = jnp.arange(s).astype(F32)[:, None] * inv[None, :]
    cos, sin = jnp.cos(ang), jnp.sin(ang)
    return jnp.concatenate([cos, cos], axis=-1), jnp.concatenate([-sin, sin], axis=-1)


def kernel(x, p, norm_mix_pre, norm_mix_post, norm_ffn_pre, norm_ffn_post, ev_w_in, ev_w_out, pool_w, pool_scale, gla_gate_w2, gla_gate_b, gla_norm, od_w_in, od_w_out, rwkv_mu, rwkv_w0, rwkv_w2, rwkv_a0, rwkv_a2, rwkv_g2, rwkv_k_k, rwkv_k_a, rwkv_r_k, rwkv_ln_w, rwkv_ln_b, ffn_up, ffn_down, ple_proj, ple_gate, ple_norm):
    batch, s, d = x.shape
    depth = p.shape[0]
    m = batch * s
    xf = x.reshape(m, d)
    cos2, sin2 = _rope_tables(s)
    up_all, down_all = ffn_up.astype(BF16), ffn_down.astype(BF16)
    gate_all, proj_all = ple_gate.astype(BF16), ple_proj.astype(BF16)
    p_all = p.reshape(depth, m, PLE_DIM)
    for i in range(depth):
        j = i // 2
        if i % 2 == 0:
            xf = _even_mixer(xf, norm_mix_pre[i][None], norm_mix_post[i][None], ev_w_in[j], ev_w_out[j],
                             pool_w[j], pool_scale[j], gla_gate_w2[j], gla_gate_b[j], gla_norm[j], batch)
        else:
            xf = _odd_mixer(xf, norm_mix_pre[i][None], norm_mix_post[i][None], od_w_in[j], od_w_out[j],
                            rwkv_mu[j], rwkv_w0[j], rwkv_w2[j], rwkv_a0[j], rwkv_a2[j], rwkv_g2[j],
                            rwkv_k_k[j], rwkv_k_a[j], rwkv_r_k[j], rwkv_ln_w[j], rwkv_ln_b[j],
                            cos2, sin2, batch)
        xf = _ffn(xf, norm_ffn_pre[i][None], norm_ffn_post[i][None], up_all, down_all, i)
        xf = _ple(xf, ple_norm[i][None], gate_all, p_all, proj_all, i)
    return xf.reshape(batch, s, d)
```

```python
import functools
import math

import numpy as np
import jax
import jax.numpy as jnp
from jax import lax
from jax.experimental import pallas as pl
from jax.experimental.pallas import tpu as pltpu

F32 = jnp.float32
BF16 = jnp.bfloat16
MIX_DTYPE = BF16
ACT_DTYPE = BF16

NORM_EPS = 1e-6
LOG2E = math.log2(math.e)
LANES = 128
BF16_ROWS = 16
VMEM_LIMIT_BYTES = 56 * 1024 * 1024

D_MODEL = 2048
PLE_DIM = 256
POOL_WINDOWS = (2, 4, 8, 16)
POOL_GROUP = 128
POOL_WIDTH = 512
POOL_HALO = 16
GLA_HEADS = 4
GLA_DK = 192
GLA_DV = 384
GLA_QK = GLA_HEADS * GLA_DK
GLA_WIDTH = GLA_HEADS * GLA_DV
GLA_GATE_RANK = 16
GLA_TAU = 16.0
RWKV_HEAD = 64
RWKV_WIDTH = 1024
RWKV_DECAY_RANK = 64
RWKV_A_RANK = 64
RWKV_GATE_RANK = 160
RWKV_LR = RWKV_DECAY_RANK + RWKV_A_RANK + RWKV_GATE_RANK
RWKV_LR_PAD = 384
RWKV_LN_EPS = 64e-5
DIL_WIDTH = 1024
DIL_HEAD = 128
DIL_HEADS = 8
DIL_PATTERNS = ((128, 1), (512, 4), (2048, 16))
DIL_BLOCK = 128
ROPE_THETA = 10000.0

ROW_CHUNK = 256
PROJ_TM = 1024
MIX_TM = 1024
FFN_TM = 512
FFN_TF = 1024
PLE_TM = 1024
POOL_TS = 512
GLA_CHUNK = 64
GLA_TS = 1024
GLA_GROUP = 8
RWKV_CHUNK = 64
RWKV_TS = 1024
RWKV_GROUP = 16
DIL_TS = 2048


def _params(*sem):
    return pltpu.CompilerParams(dimension_semantics=sem, vmem_limit_bytes=VMEM_LIMIT_BYTES)


def _dot(a, b):
    return jnp.dot(a.astype(BF16), b.astype(BF16), preferred_element_type=F32)


def _dot_nt(a, b):
    return lax.dot_general(a.astype(BF16), b.astype(BF16), (((1,), (1,)), ((), ())),
                           preferred_element_type=F32)


def _dot_tn(a, b):
    return lax.dot_general(a.astype(BF16), b.astype(BF16), (((0,), (0,)), ((), ())),
                           preferred_element_type=F32)


def _split(x):
    hi = x.astype(BF16)
    return hi, (x - hi.astype(F32)).astype(BF16)


def _dot3_general(a, b, dims):
    ah, al = _split(a)
    bh, bl = _split(b)
    d = lambda u, v: lax.dot_general(u, v, (dims, ((), ())), preferred_element_type=F32)
    return d(ah, bh) + d(ah, bl) + d(al, bh)


def _dot3(a, b):
    return _dot3_general(a, b, ((1,), (0,)))


def _dot3_nt(a, b):
    return _dot3_general(a, b, ((1,), (1,)))


def _dot_sel(sel_bf16, x):
    hi = x.astype(BF16)
    r1 = x - hi.astype(F32)
    mid = r1.astype(BF16)
    lo = (r1 - mid.astype(F32)).astype(BF16)
    out = jnp.dot(sel_bf16, hi, preferred_element_type=F32)
    out = out + jnp.dot(sel_bf16, mid, preferred_element_type=F32)
    return out + jnp.dot(sel_bf16, lo, preferred_element_type=F32)


def _row_chunks(rows):
    step = min(ROW_CHUNK, rows)
    return [pl.ds(r, step) for r in range(0, rows, step)]


def _rms(x, gain):
    return x * lax.rsqrt(jnp.mean(x * x, axis=-1, keepdims=True) + NORM_EPS) * gain


def _sigmoid(x):
    return 1.0 / (1.0 + jnp.exp(-x))


def _log_sigmoid(x):
    return jnp.minimum(x, 0.0) - jnp.log1p(jnp.exp(-jnp.abs(x)))


def _proj_kernel(starts, x_ref, gain_ref, *refs):
    nw = len(starts) - 1
    w_refs, o_refs, h_ref = refs[:nw], refs[nw:2 * nw], refs[2 * nw]
    j = pl.program_id(1)

    @pl.when(j == 0)
    def _():
        for rows in _row_chunks(x_ref.shape[0]):
            h = _rms(x_ref[rows, :], gain_ref[...]).astype(BF16)
            h_ref[rows, :] = h
            o_refs[0][rows, :] = jnp.dot(h, w_refs[0][...], preferred_element_type=F32).astype(o_refs[0].dtype)

    for k in range(nw):
        @pl.when((j >= max(starts[k], 1)) & (j < starts[k + 1]))
        def _(k=k):
            o_refs[k][...] = jnp.dot(h_ref[...], w_refs[k][...],
                                     preferred_element_type=F32).astype(o_refs[k].dtype)


def _proj(x, gain, ws, dtypes):
    m, d = x.shape
    tm = min(PROJ_TM, m)
    starts = [0]
    for w in ws:
        starts.append(starts[-1] + w.shape[0])

    def group(k):
        return lambda i, j: jnp.clip(j - starts[k], 0, ws[k].shape[0] - 1)

    w_specs = [pl.BlockSpec((None, d, w.shape[2]), lambda i, j, gk=group(k): (gk(i, j), 0, 0))
               for k, w in enumerate(ws)]
    o_specs = [pl.BlockSpec((None, tm, w.shape[2]), lambda i, j, gk=group(k): (gk(i, j), i, 0))
               for k, w in enumerate(ws)]
    return pl.pallas_call(
        functools.partial(_proj_kernel, tuple(starts)),
        grid=(m // tm, starts[-1]),
        in_specs=[pl.BlockSpec((tm, d), lambda i, j: (i, 0)), pl.BlockSpec((1, d), lambda i, j: (0, 0))] + w_specs,
        out_specs=o_specs,
        out_shape=[jax.ShapeDtypeStruct((w.shape[0], m, w.shape[2]), dt) for w, dt in zip(ws, dtypes)],
        scratch_shapes=[pltpu.VMEM((tm, d), BF16)],
        compiler_params=_params("parallel", "arbitrary"),
        name="norm_proj",
    )(x, gain, *ws)


def _mix_out_kernel(a1_ref, a2_ref, w1_ref, w2_ref, x_ref, gain_ref, o_ref):
    for rows in _row_chunks(x_ref.shape[0]):
        y = _dot(a1_ref[rows, :], w1_ref[...]) + _dot(a2_ref[rows, :], w2_ref[...])
        o_ref[rows, :] = x_ref[rows, :] + _rms(y, gain_ref[...])


def _resident(shape, index_map):
    return pl.BlockSpec(shape, index_map, pipeline_mode=pl.Buffered(1))


def _mix_out(a1, a2, w1, w2, x, gain):
    m, d = x.shape
    k1, k2 = a1.shape[1], a2.shape[1]
    tm = min(MIX_TM, m)
    return pl.pallas_call(
        _mix_out_kernel,
        grid=(m // tm,),
        in_specs=[
            pl.BlockSpec((tm, k1), lambda i: (i, 0)),
            pl.BlockSpec((tm, k2), lambda i: (i, 0)),
            _resident((k1, d), lambda i: (0, 0)),
            _resident((k2, d), lambda i: (0, 0)),
            pl.BlockSpec((tm, d), lambda i: (i, 0)),
            pl.BlockSpec((1, d), lambda i: (0, 0)),
        ],
        out_specs=pl.BlockSpec((tm, d), lambda i: (i, 0)),
        out_shape=jax.ShapeDtypeStruct((m, d), F32),
        compiler_params=_params("parallel"),
        name="mix_out",
    )(a1, a2, w1, w2, x, gain)


def _ffn_kernel(x_ref, gpre_ref, gpost_ref, up_ref, down_ref, o_ref, h_ref, acc_ref):
    f = pl.program_id(1)
    last = pl.num_programs(1) - 1

    def contribution(h):
        a = jnp.maximum(jnp.dot(h, up_ref[...], preferred_element_type=F32), 0.0)
        return jnp.dot((a * a).astype(BF16), down_ref[...], preferred_element_type=F32)

    @pl.when(f == 0)
    def _():
        for rows in _row_chunks(x_ref.shape[0]):
            h = _rms(x_ref[rows, :], gpre_ref[...]).astype(BF16)
            h_ref[rows, :] = h
            acc_ref[rows, :] = contribution(h)

    @pl.when((f > 0) & (f < last))
    def _():
        acc_ref[...] += contribution(h_ref[...])

    @pl.when(f == last)
    def _():
        for rows in _row_chunks(x_ref.shape[0]):
            y = acc_ref[rows, :] + contribution(h_ref[rows, :])
            o_ref[rows, :] = x_ref[rows, :] + _rms(y, gpost_ref[...])


def _ffn(x, gpre, gpost, up, down, layer):
    m, d = x.shape
    dff = up.shape[2]
    tm = min(FFN_TM, m)
    tf = min(FFN_TF, dff)
    assert dff // tf >= 2
    return pl.pallas_call(
        _ffn_kernel,
        grid=(m // tm, dff // tf),
        in_specs=[
            pl.BlockSpec((tm, d), lambda i, f: (i, 0)),
            pl.BlockSpec((1, d), lambda i, f: (0, 0)),
            pl.BlockSpec((1, d), lambda i, f: (0, 0)),
            pl.BlockSpec((None, d, tf), lambda i, f: (layer, 0, f)),
            pl.BlockSpec((None, tf, d), lambda i, f: (layer, f, 0)),
        ],
        out_specs=pl.BlockSpec((tm, d), lambda i, f: (i, 0)),
        out_shape=jax.ShapeDtypeStruct((m, d), F32),
        scratch_shapes=[pltpu.VMEM((tm, d), BF16), pltpu.VMEM((tm, d), F32)],
        compiler_params=_params("parallel", "arbitrary"),
        name="ffn",
    )(x, gpre, gpost, up, down)


def _ple_kernel(x_ref, gain_ref, wg_ref, p_ref, wp_ref, o_ref):
    for rows in _row_chunks(x_ref.shape[0]):
        x = x_ref[rows, :]
        gate = _sigmoid(_dot(_rms(x, gain_ref[...]), wg_ref[...]))
        o_ref[rows, :] = x + _dot(p_ref[rows, :], wp_ref[...]) * gate


def _ple(x, gain, wg, p, wp, layer):
    m, d = x.shape
    pd = p.shape[2]
    tm = min(PLE_TM, m)
    return pl.pallas_call(
        _ple_kernel,
        grid=(m // tm,),
        in_specs=[
            pl.BlockSpec((tm, d), lambda i: (i, 0)),
            pl.BlockSpec((1, d), lambda i: (0, 0)),
            _resident((None, d, d), lambda i: (layer, 0, 0)),
            pl.BlockSpec((None, tm, pd), lambda i: (layer, i, 0)),
            _resident((None, pd, d), lambda i: (layer, 0, 0)),
        ],
        out_specs=pl.BlockSpec((tm, d), lambda i: (i, 0)),
        out_shape=jax.ShapeDtypeStruct((m, d), F32),
        compiler_params=_params("parallel"),
        name="ple",
    )(x, gain, wg, p, wp)


def _pool_kernel(u_ref, prev_ref, w_ref, scale_ref, o_ref, ext_ref):
    i = pl.program_id(1)
    ts = u_ref.shape[0]
    ext_ref[pl.ds(POOL_HALO, ts), :] = u_ref[...]

    @pl.when(i == 0)
    def _():
        ext_ref[pl.ds(0, POOL_HALO), :] = jnp.zeros((POOL_HALO, POOL_WIDTH), F32)

    @pl.when(i > 0)
    def _():
        ext_ref[pl.ds(0, POOL_HALO), :] = prev_ref[...]

    t = i * ts + lax.broadcasted_iota(jnp.int32, (ts, 1), 0)
    for gi, w in enumerate(POOL_WINDOWS):
        lo = gi * POOL_GROUP
        cur = ext_ref[pl.ds(POOL_HALO, ts), pl.ds(lo, POOL_GROUP)]
        tot = cur
        for j in range(1, w):
            tot = tot + ext_ref[pl.ds(POOL_HALO - j, ts), pl.ds(lo, POOL_GROUP)]
        cnt = jnp.minimum(t + 1, w).astype(F32)
        pooled = tot / cnt - cur
        out = _dot(pooled, w_ref[gi]) * scale_ref[:, pl.ds(lo, POOL_GROUP)]
        o_ref[:, pl.ds(lo, POOL_GROUP)] = out.astype(o_ref.dtype)


def _pool(u, pool_w, pool_scale, batch):
    m, width = u.shape
    s = m // batch
    ts = min(POOL_TS, s)
    nt = s // ts
    hb = ts // POOL_HALO
    return pl.pallas_call(
        _pool_kernel,
        grid=(batch, nt),
        in_specs=[
            pl.BlockSpec((ts, width), lambda b, i: (b * nt + i, 0)),
            pl.BlockSpec((POOL_HALO, width), lambda b, i: (jnp.maximum((b * nt + i) * hb - 1, 0), 0)),
            pl.BlockSpec((len(POOL_WINDOWS), POOL_GROUP, POOL_GROUP), lambda b, i: (0, 0, 0)),
            pl.BlockSpec((1, width), lambda b, i: (0, 0)),
        ],
        out_specs=pl.BlockSpec((ts, width), lambda b, i: (b * nt + i, 0)),
        out_shape=jax.ShapeDtypeStruct((m, width), MIX_DTYPE),
        scratch_shapes=[pltpu.VMEM((POOL_HALO + ts, width), F32)],
        compiler_params=_params("parallel", "arbitrary"),
        name="pool_mixer",
    )(u, u, pool_w, pool_scale)


def _gla_constants(c):
    nlev = int(math.log2(c))
    i = np.arange(c)[:, None]
    j = np.arange(c)[None, :]
    x = i ^ j
    level = np.full((c, c), -1, np.int32)
    for lev in range(nlev):
        level = np.where((x >> lev) == 1, lev, level)
    level = np.where(j > i, -1, level)
    level = np.where(i == j, nlev, level)
    return jnp.asarray(j <= i, BF16), jnp.asarray(level, jnp.int32)


def _gla_kernel(z_ref, glr_ref, w2_ref, gb_ref, gn_ref, tri_ref, lev_ref, o_ref, st_ref, cum_ref):
    c = GLA_CHUNK
    nlev = int(math.log2(c))
    ts = z_ref.shape[0]
    sub = 8
    kq, kk, kv, kg = 0, GLA_DK, 2 * GLA_DK, 2 * GLA_DK + GLA_DV

    @pl.when(pl.program_id(2) == 0)
    def _():
        st_ref[...] = jnp.zeros_like(st_ref)

    tri = tri_ref[...]
    level = lev_ref[...]
    gn = gn_ref[...]
    row8 = lax.broadcasted_iota(jnp.int32, (sub, GLA_DK), 0)

    g_all = _log_sigmoid(_dot3(glr_ref[...], w2_ref[...]) + gb_ref[...]) * (LOG2E / GLA_TAU)
    for ci in range(ts // c):
        cum_ref[pl.ds(ci * c, c), :] = _dot_sel(tri, g_all[ci * c:(ci + 1) * c])

    def cum_row(r):
        return jnp.broadcast_to(cum_ref[pl.ds(r, 1), :], (sub, GLA_DK))

    def level_exponent(ci, lev):
        s = 1 << lev
        tiles = []
        for a in range(c // sub):
            r0 = ci * c + a * sub
            b = cum_ref[pl.ds(r0, sub), :]
            if 2 * s >= sub:
                bm = cum_row(ci * c + (a * sub // (2 * s)) * (2 * s) + s - 1)
            else:
                bm = cum_row(r0 + s - 1)
                for blk in range(2 * s, sub, 2 * s):
                    bm = jnp.where(row8 >= blk, cum_row(r0 + blk + s - 1), bm)
            tiles.append(-jnp.abs(b - bm))
        return jnp.concatenate(tiles, axis=0)

    st = st_ref[...]
    for gi in range(ts // (c * GLA_GROUP)):
        ids = range(gi * GLA_GROUP, (gi + 1) * GLA_GROUP)
        rows = {ci: pl.ds(ci * c, c) for ci in ids}
        q = {ci: z_ref[rows[ci], kq:kq + GLA_DK].astype(F32) * (GLA_DK ** -0.5) for ci in ids}
        k = {ci: z_ref[rows[ci], kk:kk + GLA_DK].astype(F32) for ci in ids}
        b = {ci: cum_ref[rows[ci], :] for ci in ids}
        b_last = {ci: cum_ref[pl.ds(ci * c + c - 1, 1), :] for ci in ids}
        att = {ci: jnp.where(level == nlev, _dot_nt(q[ci], k[ci]), 0.0) for ci in ids}
        for lev in range(nlev):
            el = {ci: jnp.exp2(level_exponent(ci, lev)) for ci in ids}
            att = {ci: jnp.where(level == lev, _dot_nt(q[ci] * el[ci], k[ci] * el[ci]), att[ci]) for ci in ids}
        intra = {ci: _dot(att[ci], z_ref[rows[ci], kv:kv + GLA_DV]) for ci in ids}
        qe = {ci: q[ci] * jnp.exp2(b[ci]) for ci in ids}
        ke = {ci: k[ci] * jnp.exp2(b_last[ci] - b[ci]) for ci in ids}
        for ci in ids:
            o = _dot_nt(qe[ci], st) + intra[ci]
            st = st * jnp.exp2(b_last[ci]) + _dot_tn(z_ref[rows[ci], kv:kv + GLA_DV], ke[ci])
            o = o * lax.rsqrt(jnp.mean(o * o, axis=-1, keepdims=True) + NORM_EPS) * gn
            gate = z_ref[rows[ci], kg:kg + GLA_DV].astype(F32)
            o_ref[rows[ci], :] = (o * (gate * _sigmoid(gate))).astype(o_ref.dtype)
    st_ref[...] = st


def _gla(z, glr, w2p, gate_b, gla_norm, batch):
    h, m, zw = z.shape
    s = m // batch
    ts = min(GLA_TS, s)
    nt = s // ts
    tri, level = _gla_constants(GLA_CHUNK)
    row = lambda b, hh, i: b * nt + i
    return pl.pallas_call(
        _gla_kernel,
        grid=(batch, h, nt),
        in_specs=[
            pl.BlockSpec((None, ts, zw), lambda b, hh, i: (hh, row(b, hh, i), 0)),
            pl.BlockSpec((ts, LANES), lambda b, hh, i: (row(b, hh, i), 0)),
            pl.BlockSpec((None, LANES, GLA_DK), lambda b, hh, i: (hh, 0, 0)),
            pl.BlockSpec((None, 1, GLA_DK), lambda b, hh, i: (hh, 0, 0)),
            pl.BlockSpec((1, GLA_DV), lambda b, hh, i: (0, 0)),
            pl.BlockSpec(tri.shape, lambda b, hh, i: (0, 0)),
            pl.BlockSpec(level.shape, lambda b, hh, i: (0, 0)),
        ],
        out_specs=pl.BlockSpec((ts, GLA_DV), lambda b, hh, i: (row(b, hh, i), hh)),
        out_shape=jax.ShapeDtypeStruct((m, GLA_WIDTH), MIX_DTYPE),
        scratch_shapes=[pltpu.VMEM((GLA_DV, GLA_DK), F32), pltpu.VMEM((ts, GLA_DK), F32)],
        compiler_params=_params("parallel", "parallel", "arbitrary"),
        name="gla",
    )(z, glr, w2p, gate_b, gla_norm, tri, level)


def _head_sum_matrix():
    lane = np.arange(LANES)
    return jnp.asarray((lane[:, None] // RWKV_HEAD) == (lane[None, :] // RWKV_HEAD), BF16)


def _shifted(cur, prev_rows, first):
    last = prev_rows.shape[0] - 1
    prev_row = jnp.where(first, 0.0, prev_rows[last:last + 1, :])
    rolled = pltpu.roll(cur, 1, 0)
    row = lax.broadcasted_iota(jnp.int32, cur.shape, 0)
    return jnp.where(row == 0, prev_row, rolled)


def _rwkv_prepare(first, rkv_ref, rkvp_ref, lr_ref, lrp_ref, mu_ref, mulr_ref, w0_ref, w2_ref, a0_ref,
                  a2_ref, g2_ref, kk_ref, ka_ref, hs_ref,
                  r_out, lw_out, k_out, v_out, a_out, b_out, g_out):
    def mixed(cur, prev_rows, mu):
        cur, prev_rows = cur.astype(F32), prev_rows.astype(F32)
        return cur + (_shifted(cur, prev_rows, first) - cur) * mu

    r = mixed(rkv_ref[0], rkvp_ref[0], mu_ref[0])
    k = mixed(rkv_ref[1], rkvp_ref[1], mu_ref[1])
    v = mixed(rkv_ref[2], rkvp_ref[2], mu_ref[2])
    lr = mixed(lr_ref[...], lrp_ref[...], mulr_ref[...])

    lr_wa = lr[:, 0:LANES]
    lr_g = lr[:, LANES:RWKV_LR_PAD]
    w_log = _log_sigmoid(w0_ref[...] + _dot(jnp.tanh(lr_wa), w2_ref[...])) - 0.5
    lw = -jnp.exp(w_log) * LOG2E
    a = _sigmoid(a0_ref[...] + _dot(lr_wa, a2_ref[...]))
    g = _dot(_sigmoid(lr_g), g2_ref[...])

    kk = k * kk_ref[...]
    norm = jnp.sqrt(_dot(kk * kk, hs_ref[...]))
    kk = kk / jnp.maximum(norm, 1e-12)
    r_out[...] = r
    lw_out[...] = lw
    k_out[...] = k * (1.0 + (a - 1.0) * ka_ref[...])
    v_out[...] = v
    a_out[...] = -kk
    b_out[...] = kk * a
    g_out[...] = g


def _rwkv_constants(c):
    i = np.arange(c)[:, None]
    j = np.arange(c)[None, :]
    tri = jnp.asarray(j <= i, BF16)
    i2 = np.arange(2 * c)[:, None] % c
    j2 = np.arange(2 * c)[None, :] % c
    strict = jnp.asarray(i2 > j2, F32)
    incl = jnp.asarray(i2 >= j2, F32)
    return tri, strict, incl


def _rwkv_kernel(rkv_ref, rkvp_ref, lr_ref, lrp_ref, mu_ref, mulr_ref, w0_ref, w2_ref, a0_ref, a2_ref, g2_ref,
                 kk_ref, ka_ref, lnw_ref, lnb_ref, rk_ref, tri_ref, strict_ref, incl_ref, hs_ref, o_ref,
                 state_ref, p_ref, gm_ref, rq_ref, y0_ref, r_ref, lw_ref, k_ref, v_ref, a_ref, b_ref, g_ref, y_ref):
    c = RWKV_CHUNK
    nlev = int(math.log2(c))
    ts = o_ref.shape[0]
    nchunk = ts // c
    first = pl.program_id(2) == 0

    @pl.when(first)
    def _():
        state_ref[...] = jnp.zeros_like(state_ref)

    _rwkv_prepare(first, rkv_ref, rkvp_ref, lr_ref, lrp_ref, mu_ref, mulr_ref, w0_ref, w2_ref, a0_ref, a2_ref,
                  g2_ref, kk_ref, ka_ref, hs_ref, r_ref, lw_ref, k_ref, v_ref, a_ref, b_ref, g_ref)

    lane = lax.broadcasted_iota(jnp.int32, (1, LANES), 1)
    head0 = lane < RWKV_HEAD
    tri = tri_ref[...]
    strict = strict_ref[...]
    incl = incl_ref[...]
    eye = (lax.broadcasted_iota(jnp.int32, (LANES, LANES), 0)
           == lax.broadcasted_iota(jnp.int32, (LANES, LANES), 1)).astype(F32)

    def stack(x):
        return jnp.concatenate([jnp.where(head0, x, 0.0), jnp.where(head0, 0.0, x)], axis=0)

    grp = RWKV_GROUP
    each = range(grp)

    def local(gi):
        base = gi * grp
        rows = [pl.ds((base + t) * c, c) for t in each]
        lw = [lw_ref[rows[t], :] for t in each]
        cum = [_dot_sel(tri, lw[t]) for t in each]
        last = [cum[t][c - 1:c] for t in each]
        grow = [jnp.exp2(-cum[t]) for t in each]
        fall = [jnp.exp2(last[t] - cum[t]) for t in each]
        at2 = [stack(a_ref[rows[t], :] * jnp.exp2(cum[t] - lw[t])) for t in each]
        rt2 = [stack(r_ref[rows[t], :] * jnp.exp2(cum[t])) for t in each]
        bt2 = [stack(b_ref[rows[t], :] * grow[t]) for t in each]
        kt2 = [stack(k_ref[rows[t], :] * grow[t]) for t in each]
        bh2 = [stack(b_ref[rows[t], :] * fall[t]) for t in each]
        kh2 = [stack(k_ref[rows[t], :] * fall[t]) for t in each]
        v2 = [stack(v_ref[rows[t], :]) for t in each]
        big = [_dot_nt(jnp.concatenate([at2[t], rt2[t]], axis=0), jnp.concatenate([bt2[t], kt2[t]], axis=0))
               for t in each]
        a_ab = [big[t][0:2 * c, 0:2 * c] * strict for t in each]
        a_ak = [big[t][0:2 * c, 2 * c:4 * c] * strict for t in each]
        a_rb = [big[t][2 * c:4 * c, 0:2 * c] * incl for t in each]
        a_rk = [big[t][2 * c:4 * c, 2 * c:4 * c] * incl for t in each]
        rhs = [jnp.concatenate([at2[t], _dot(a_ak[t], v2[t])], axis=1) for t in each]
        n = a_ab
        for lev in range(nlev):
            rhs = [rhs[t] + _dot(n[t], rhs[t]) for t in each]
            if lev + 1 < nlev:
                n = [_dot(n[t], n[t]) for t in each]
        for t in each:
            w2 = rhs[t][:, 0:LANES]
            z2 = rhs[t][:, LANES:2 * LANES]
            gamma = jnp.exp2(last[t])
            p_ref[base + t] = eye * gamma + _dot_tn(w2, bh2[t])
            gm_ref[base + t] = _dot_tn(jnp.concatenate([z2, v2[t]], axis=0),
                                       jnp.concatenate([bh2[t], kh2[t]], axis=0))
            rq_ref[base + t] = rt2[t] + _dot(a_rb[t], w2)
            y0_ref[base + t] = _dot(a_rb[t], z2) + _dot(a_rk[t], v2[t])

    st = state_ref[...]
    for gi in range(nchunk // grp):
        local(gi)
        for ci in range(gi * grp, (gi + 1) * grp):
            y2 = _dot_nt(rq_ref[ci], st) + y0_ref[ci]
            y_ref[pl.ds(ci * c, c), :] = y2[0:c] + y2[c:2 * c]
            st = _dot3(st, p_ref[ci]) + gm_ref[ci]
    state_ref[...] = st

    hs = hs_ref[...]
    y = y_ref[...]
    mean = _dot(y, hs) * (1.0 / RWKV_HEAD)
    yc = y - mean
    var = _dot(yc * yc, hs) * (1.0 / RWKV_HEAD)
    yn = yc * lax.rsqrt(var + RWKV_LN_EPS) * lnw_ref[...] + lnb_ref[...]
    bonus = _dot(r_ref[...] * k_ref[...] * rk_ref[...], hs) * v_ref[...]
    o_ref[...] = ((yn + bonus) * g_ref[...]).astype(o_ref.dtype)


def _rwkv(z6, lr, mu_rkv, mu_lr, w0, w2p, a0, a2p, g2p, k_k, k_a, ln_w, ln_b, r_k, batch):
    m = lr.shape[0]
    s = m // batch
    ts = min(RWKV_TS, s)
    nt = s // ts
    nb = RWKV_WIDTH // LANES
    c = RWKV_CHUNK
    nchunk = ts // c
    tri, strict, incl = _rwkv_constants(c)
    hs = _head_sum_matrix()
    row = lambda bb, j, i: bb * nt + i
    prow = lambda bb, j, i: jnp.maximum((bb * nt + i) * (ts // 8) - 1, 0)
    vecspec = pl.BlockSpec((1, LANES), lambda bb, j, i: (0, j))
    lrw = pl.BlockSpec((LANES, LANES), lambda bb, j, i: (0, j))
    lrg = pl.BlockSpec((RWKV_LR_PAD - LANES, LANES), lambda bb, j, i: (0, j))
    const = lambda arr: pl.BlockSpec(arr.shape, lambda bb, j, i: (0,) * arr.ndim)
    tile = pltpu.VMEM((ts, LANES), F32)
    return pl.pallas_call(
        _rwkv_kernel,
        grid=(batch, nb, nt),
        in_specs=[
            pl.BlockSpec((3, ts, LANES), lambda bb, j, i: (0, row(bb, j, i), j)),
            pl.BlockSpec((3, BF16_ROWS, LANES),
                         lambda bb, j, i: (0, jnp.maximum((bb * nt + i) * (ts // BF16_ROWS) - 1, 0), j)),
            pl.BlockSpec((ts, RWKV_LR_PAD), lambda bb, j, i: (row(bb, j, i), 0)),
            pl.BlockSpec((8, RWKV_LR_PAD), lambda bb, j, i: (prow(bb, j, i), 0)),
            pl.BlockSpec((3, 1, LANES), lambda bb, j, i: (0, 0, j)),
            pl.BlockSpec((1, RWKV_LR_PAD), lambda bb, j, i: (0, 0)),
            vecspec, lrw, vecspec, lrw, lrg, vecspec, vecspec,
            vecspec, vecspec, vecspec,
            const(tri), const(strict), const(incl), const(hs),
        ],
        out_specs=pl.BlockSpec((ts, LANES), lambda bb, j, i: (row(bb, j, i), j)),
        out_shape=jax.ShapeDtypeStruct((m, RWKV_WIDTH), MIX_DTYPE),
        scratch_shapes=[
            pltpu.VMEM((LANES, LANES), F32),
            pltpu.VMEM((nchunk, LANES, LANES), F32),
            pltpu.VMEM((nchunk, LANES, LANES), F32),
            pltpu.VMEM((nchunk, 2 * c, LANES), F32),
            pltpu.VMEM((nchunk, 2 * c, LANES), F32),
        ] + [tile] * 8,
        compiler_params=_params("parallel", "parallel", "arbitrary"),
        name="rwkv",
    )(z6, z6, lr, lr, mu_rkv, mu_lr, w0, w2p, a0, a2p, g2p, k_k, k_a, ln_w, ln_b, r_k, tri, strict, incl, hs)


def _rope(x, cos2, sin2):
    return x * cos2 + pltpu.roll(x, DIL_HEAD // 2, 1) * sin2


def _dil_kernel(q_ref, kp_ref, kc_ref, vp_ref, vc_ref, cos_ref, sin_ref, cosp_ref, sinp_ref, o_ref,
                qs_ref, ks_ref, vs_ref, acc_ref, m_ref, l_ref):
    ts = q_ref.shape[0]
    i = pl.program_id(2)
    blk = DIL_BLOCK
    qs_ref[...] = _rope(q_ref[...].astype(F32), cos_ref[...], sin_ref[...]) * (DIL_HEAD ** -0.5)
    ks_ref[pl.ds(0, ts), :] = _rope(kp_ref[...].astype(F32), cosp_ref[...], sinp_ref[...])
    ks_ref[pl.ds(ts, ts), :] = _rope(kc_ref[...].astype(F32), cos_ref[...], sin_ref[...])
    vs_ref[pl.ds(0, ts), :] = vp_ref[...].astype(F32)
    vs_ref[pl.ds(ts, ts), :] = vc_ref[...].astype(F32)

    qi = lax.broadcasted_iota(jnp.int32, (blk, 2 * blk), 0)
    ki = lax.broadcasted_iota(jnp.int32, (blk, 2 * blk), 1)
    dist = qi + blk - ki
    band = (dist >= 0) & (dist <= blk)

    order = sorted(range(len(DIL_PATTERNS)), key=lambda pi: -DIL_PATTERNS[pi][1])
    stored = order[:-1]
    assert DIL_PATTERNS[order[-1]][1] == 1

    for pi in order:
        window, dil = DIL_PATTERNS[pi]
        assert window // dil == blk
        per_res = ts // (blk * dil)

        def unit(u, carry, pi=pi, dil=dil, per_res=per_res):
            res = u // per_res
            nblk = u % per_res
            qstart = res + nblk * (blk * dil)
            kstart = ts + qstart - blk * dil
            q = qs_ref[pl.ds(qstart, blk, stride=dil), :]
            k = ks_ref[pl.ds(kstart, 2 * blk, stride=dil), :]
            v = vs_ref[pl.ds(kstart, 2 * blk, stride=dil), :]
            s = _dot_nt(q, k)
            key_pos = i * ts + (qstart - blk * dil) + ki * dil
            s = jnp.where(band & (key_pos >= 0), s, -jnp.inf)
            mx = jnp.max(s, axis=-1, keepdims=True)
            p = jnp.exp(s - mx)
            acc = _dot(p, v)
            mxb = jnp.broadcast_to(mx, (blk, LANES))
            lb = jnp.broadcast_to(jnp.sum(p, axis=-1, keepdims=True), (blk, LANES))
            if pi in stored:
                slot = stored.index(pi)
                acc_ref[slot, pl.ds(qstart, blk, stride=dil), :] = acc
                m_ref[slot, pl.ds(qstart, blk, stride=dil), :] = mxb
                l_ref[slot, pl.ds(qstart, blk, stride=dil), :] = lb
            else:
                rows = pl.ds(pl.multiple_of(qstart, blk), blk)
                m_all = mxb
                for slot in range(len(stored)):
                    m_all = jnp.maximum(m_all, m_ref[slot, rows, :])
                cf = jnp.exp(mxb - m_all)
                num = cf * acc
                den = cf * lb
                for slot in range(len(stored)):
                    cf = jnp.exp(m_ref[slot, rows, :] - m_all)
                    num = num + cf * acc_ref[slot, rows, :]
                    den = den + cf * l_ref[slot, rows, :]
                o_ref[rows, :] = (num / den).astype(o_ref.dtype)
            return carry

        lax.fori_loop(0, ts // blk, unit, 0, unroll=8)


def _dilated(z6, cos2, sin2, batch):
    m = z6.shape[1]
    s = m // batch
    ts = DIL_TS
    assert s % ts == 0
    nt = s // ts
    cur = lambda which: pl.BlockSpec((None, ts, DIL_HEAD), lambda b, h, i: (which, b * nt + i, h))
    prev = lambda which: pl.BlockSpec((None, ts, DIL_HEAD),
                                      lambda b, h, i: (which, b * nt + jnp.maximum(i - 1, 0), h))
    tab = pl.BlockSpec((ts, DIL_HEAD), lambda b, h, i: (i, 0))
    tabp = pl.BlockSpec((ts, DIL_HEAD), lambda b, h, i: (jnp.maximum(i - 1, 0), 0))
    npat = len(DIL_PATTERNS) - 1
    return pl.pallas_call(
        _dil_kernel,
        grid=(batch, DIL_HEADS, nt),
        in_specs=[cur(3), prev(4), cur(4), prev(5), cur(5), tab, tab, tabp, tabp],
        out_specs=pl.BlockSpec((ts, DIL_HEAD), lambda b, h, i: (b * nt + i, h)),
        out_shape=jax.ShapeDtypeStruct((m, DIL_WIDTH), MIX_DTYPE),
        scratch_shapes=[
            pltpu.VMEM((ts, DIL_HEAD), F32),
            pltpu.VMEM((2 * ts, DIL_HEAD), F32),
            pltpu.VMEM((2 * ts, DIL_HEAD), F32),
            pltpu.VMEM((npat, ts, DIL_HEAD), F32),
            pltpu.VMEM((npat, ts, LANES), F32),
            pltpu.VMEM((npat, ts, LANES), F32),
        ],
        compiler_params=_params("parallel", "parallel", "arbitrary"),
        name="dilated_attention",
    )(z6, z6, z6, z6, z6, cos2, sin2, cos2, sin2)


def _heads(w, n_heads, width):
    d = w.shape[0]
    return w.reshape(d, n_heads, width).transpose(1, 0, 2)


def _pad_rows(w, start, total):
    return jnp.zeros((total, w.shape[1]), w.dtype).at[start:start + w.shape[0]].set(w)


def _even_mixer(x, gain_pre, gain_post, w_in, w_out, pool_w, pool_scale, gate_w2, gate_b, gla_norm, batch):
    d = x.shape[1]
    o0 = POOL_WIDTH
    o1 = o0 + GLA_QK
    o2 = o1 + GLA_QK
    o3 = o2 + GLA_WIDTH
    o4 = o3 + GLA_WIDTH
    wb = w_in.astype(BF16)
    w_u = wb[:, :o0][None]
    w_z = jnp.concatenate([_heads(wb[:, o0:o1], GLA_HEADS, GLA_DK), _heads(wb[:, o1:o2], GLA_HEADS, GLA_DK),
                           _heads(wb[:, o2:o3], GLA_HEADS, GLA_DV), _heads(wb[:, o3:o4], GLA_HEADS, GLA_DV)], 2)
    w_glr = jnp.zeros((d, LANES), BF16).at[:, :GLA_GATE_RANK].set(wb[:, o4:])[None]
    u, glr, z = _proj(x, gain_pre, [w_u, w_glr, w_z], [F32, F32, ACT_DTYPE])
    u, glr = u[0], glr[0]
    a_out = _pool(u, pool_w.astype(BF16), pool_scale[None], batch)
    w2p = _heads(_pad_rows(gate_w2, 0, LANES), GLA_HEADS, GLA_DK)
    o = _gla(z, glr, w2p, gate_b.reshape(GLA_HEADS, 1, GLA_DK), gla_norm[None], batch)
    wo = w_out.astype(BF16)
    return _mix_out(a_out, o, wo[:POOL_WIDTH], wo[POOL_WIDTH:], x, gain_post)


def _odd_mixer(x, gain_pre, gain_post, w_in, w_out, mu, w0, w2, a0, a2, g2, k_k, k_a, r_k, ln_w, ln_b,
               cos2, sin2, batch):
    d = x.shape[1]
    wb = w_in.astype(BF16)
    c3 = 3 * RWKV_WIDTH
    rwkv_in = c3 + RWKV_LR
    w6 = jnp.concatenate([_heads(wb[:, :c3], 3, RWKV_WIDTH), _heads(wb[:, rwkv_in:], 3, DIL_WIDTH)], 0)
    w_lr = jnp.zeros((d, RWKV_LR_PAD), BF16).at[:, :RWKV_LR].set(wb[:, c3:rwkv_in])[None]
    z6, lr = _proj(x, gain_pre, [w6, w_lr], [ACT_DTYPE, F32])
    lr = lr[0]
    mu_rkv = mu[:c3].reshape(3, 1, RWKV_WIDTH)
    mu_lr = jnp.zeros((1, RWKV_LR_PAD), F32).at[0, :RWKV_LR].set(mu[c3:])
    w2p = _pad_rows(w2, 0, LANES).astype(BF16)
    a2p = _pad_rows(a2, RWKV_DECAY_RANK, LANES).astype(BF16)
    g2p = _pad_rows(g2, 0, RWKV_LR_PAD - LANES).astype(BF16)
    c_out = _rwkv(z6, lr, mu_rkv, mu_lr, w0[None], w2p, a0[None], a2p, g2p, k_k[None], k_a[None],
                  ln_w[None], ln_b[None], r_k.reshape(1, RWKV_WIDTH), batch)
    d_out = _dilated(z6, cos2, sin2, batch)
    wo = w_out.astype(BF16)
    return _mix_out(c_out, d_out, wo[:RWKV_WIDTH], wo[RWKV_WIDTH:], x, gain_post)


def _rope_tables(s):
    half = DIL_HEAD // 2
    inv = ROPE_THETA ** (-jnp.arange(half, dtype=F32) / half)
    ang = jnp.arange(s).astype(F32)[:, None] * inv[None, :]
    cos, sin = jnp.cos(ang), jnp.sin(ang)
    return jnp.concatenate([cos, cos], axis=-1), jnp.concatenate([-sin, sin], axis=-1)


def kernel(x, p, norm_mix_pre, norm_mix_post, norm_ffn_pre, norm_ffn_post, ev_w_in, ev_w_out, pool_w, pool_scale, gla_gate_w2, gla_gate_b, gla_norm, od_w_in, od_w_out, rwkv_mu, rwkv_w0, rwkv_w2, rwkv_a0, rwkv_a2, rwkv_g2, rwkv_k_k, rwkv_k_a, rwkv_r_k, rwkv_ln_w, rwkv_ln_b, ffn_up, ffn_down, ple_proj, ple_gate, ple_norm):
    batch, s, d = x.shape
    depth = p.shape[0]
    m = batch * s
    xf = x.reshape(m, d)
    cos2, sin2 = _rope_tables(s)
    up_all, down_all = ffn_up.astype(BF16), ffn_down.astype(BF16)
    gate_all, proj_all = ple_gate.astype(BF16), ple_proj.astype(BF16)
    p_all = p.reshape(depth, m, PLE_DIM)
    for i in range(depth):
        j = i // 2
        if i % 2 == 0:
            xf = _even_mixer(xf, norm_mix_pre[i][None], norm_mix_post[i][None], ev_w_in[j], ev_w_out[j],
                             pool_w[j], pool_scale[j], gla_gate_w2[j], gla_gate_b[j], gla_norm[j], batch)
        else:
            xf = _odd_mixer(xf, norm_mix_pre[i][None], norm_mix_post[i][None], od_w_in[j], od_w_out[j],
                            rwkv_mu[j], rwkv_w0[j], rwkv_w2[j], rwkv_a0[j], rwkv_a2[j], rwkv_g2[j],
                            rwkv_k_k[j], rwkv_k_a[j], rwkv_r_k[j], rwkv_ln_w[j], rwkv_ln_b[j],
                            cos2, sin2, batch)
        xf = _ffn(xf, norm_ffn_pre[i][None], norm_ffn_post[i][None], up_all, down_all, i)
        xf = _ple(xf, ple_norm[i][None], gate_all, p_all, proj_all, i)
    return xf.reshape(batch, s, d)
```

```python
import functools
import math

import numpy as np
import jax
import jax.numpy as jnp
from jax import lax
from jax.experimental import pallas as pl
from jax.experimental.pallas import tpu as pltpu

F32 = jnp.float32
BF16 = jnp.bfloat16
MIX_DTYPE = BF16
ACT_DTYPE = BF16

NORM_EPS = 1e-6
LOG2E = math.log2(math.e)
LANES = 128
BF16_ROWS = 16
VMEM_LIMIT_BYTES = 56 * 1024 * 1024

D_MODEL = 2048
PLE_DIM = 256
POOL_WINDOWS = (2, 4, 8, 16)
POOL_GROUP = 128
POOL_WIDTH = 512
POOL_HALO = 16
GLA_HEADS = 4
GLA_DK = 192
GLA_DV = 384
GLA_QK = GLA_HEADS * GLA_DK
GLA_WIDTH = GLA_HEADS * GLA_DV
GLA_GATE_RANK = 16
GLA_TAU = 16.0
RWKV_HEAD = 64
RWKV_WIDTH = 1024
RWKV_DECAY_RANK = 64
RWKV_A_RANK = 64
RWKV_GATE_RANK = 160
RWKV_LR = RWKV_DECAY_RANK + RWKV_A_RANK + RWKV_GATE_RANK
RWKV_LR_PAD = 384
RWKV_LN_EPS = 64e-5
DIL_WIDTH = 1024
DIL_HEAD = 128
DIL_HEADS = 8
DIL_PATTERNS = ((128, 1), (512, 4), (2048, 16))
DIL_BLOCK = 128
ROPE_THETA = 10000.0

ROW_CHUNK = 256
PROJ_TM = 1024
MIX_TM = 1024
FFN_TM = 512
FFN_TF = 1024
PLE_TM = 1024
POOL_TS = 512
GLA_CHUNK = 64
GLA_TS = 1024
GLA_GROUP = 8
RWKV_CHUNK = 64
RWKV_TS = 1024
RWKV_GROUP = 16
DIL_TS = 2048


def _params(*sem):
    return pltpu.CompilerParams(dimension_semantics=sem, vmem_limit_bytes=VMEM_LIMIT_BYTES)


def _dot(a, b):
    return jnp.dot(a.astype(BF16), b.astype(BF16), preferred_element_type=F32)


def _dot_nt(a, b):
    return lax.dot_general(a.astype(BF16), b.astype(BF16), (((1,), (1,)), ((), ())),
                           preferred_element_type=F32)


def _dot_tn(a, b):
    return lax.dot_general(a.astype(BF16), b.astype(BF16), (((0,), (0,)), ((), ())),
                           preferred_element_type=F32)


def _split(x):
    hi = x.astype(BF16)
    return hi, (x - hi.astype(F32)).astype(BF16)


def _dot3_general(a, b, dims):
    ah, al = _split(a)
    bh, bl = _split(b)
    d = lambda u, v: lax.dot_general(u, v, (dims, ((), ())), preferred_element_type=F32)
    return d(ah, bh) + d(ah, bl) + d(al, bh)


def _dot3(a, b):
    return _dot3_general(a, b, ((1,), (0,)))


def _dot3_nt(a, b):
    return _dot3_general(a, b, ((1,), (1,)))


def _dot_sel(sel_bf16, x):
    hi = x.astype(BF16)
    r1 = x - hi.astype(F32)
    mid = r1.astype(BF16)
    lo = (r1 - mid.astype(F32)).astype(BF16)
    out = jnp.dot(sel_bf16, hi, preferred_element_type=F32)
    out = out + jnp.dot(sel_bf16, mid, preferred_element_type=F32)
    return out + jnp.dot(sel_bf16, lo, preferred_element_type=F32)


def _row_chunks(rows):
    step = min(ROW_CHUNK, rows)
    return [pl.ds(r, step) for r in range(0, rows, step)]


def _rms(x, gain):
    return x * lax.rsqrt(jnp.mean(x * x, axis=-1, keepdims=True) + NORM_EPS) * gain


def _sigmoid(x):
    return 1.0 / (1.0 + jnp.exp(-x))


def _log_sigmoid(x):
    return jnp.minimum(x, 0.0) - jnp.log1p(jnp.exp(-jnp.abs(x)))


def _proj_kernel(starts, x_ref, gain_ref, *refs):
    nw = len(starts) - 1
    w_refs, o_refs, h_ref = refs[:nw], refs[nw:2 * nw], refs[2 * nw]
    j = pl.program_id(1)

    @pl.when(j == 0)
    def _():
        for rows in _row_chunks(x_ref.shape[0]):
            h = _rms(x_ref[rows, :], gain_ref[...]).astype(BF16)
            h_ref[rows, :] = h
            o_refs[0][rows, :] = jnp.dot(h, w_refs[0][...], preferred_element_type=F32).astype(o_refs[0].dtype)

    for k in range(nw):
        @pl.when((j >= max(starts[k], 1)) & (j < starts[k + 1]))
        def _(k=k):
            o_refs[k][...] = jnp.dot(h_ref[...], w_refs[k][...],
                                     preferred_element_type=F32).astype(o_refs[k].dtype)


def _proj(x, gain, ws, dtypes):
    m, d = x.shape
    tm = min(PROJ_TM, m)
    starts = [0]
    for w in ws:
        starts.append(starts[-1] + w.shape[0])

    def group(k):
        return lambda i, j: jnp.clip(j - starts[k], 0, ws[k].shape[0] - 1)

    w_specs = [pl.BlockSpec((None, d, w.shape[2]), lambda i, j, gk=group(k): (gk(i, j), 0, 0))
               for k, w in enumerate(ws)]
    o_specs = [pl.BlockSpec((None, tm, w.shape[2]), lambda i, j, gk=group(k): (gk(i, j), i, 0))
               for k, w in enumerate(ws)]
    return pl.pallas_call(
        functools.partial(_proj_kernel, tuple(starts)),
        grid=(m // tm, starts[-1]),
        in_specs=[pl.BlockSpec((tm, d), lambda i, j: (i, 0)), pl.BlockSpec((1, d), lambda i, j: (0, 0))] + w_specs,
        out_specs=o_specs,
        out_shape=[jax.ShapeDtypeStruct((w.shape[0], m, w.shape[2]), dt) for w, dt in zip(ws, dtypes)],
        scratch_shapes=[pltpu.VMEM((tm, d), BF16)],
        compiler_params=_params("parallel", "arbitrary"),
        name="norm_proj",
    )(x, gain, *ws)


def _mix_out_kernel(a1_ref, a2_ref, w1_ref, w2_ref, x_ref, gain_ref, o_ref):
    for rows in _row_chunks(x_ref.shape[0]):
        y = _dot(a1_ref[rows, :], w1_ref[...]) + _dot(a2_ref[rows, :], w2_ref[...])
        o_ref[rows, :] = x_ref[rows, :] + _rms(y, gain_ref[...])


def _resident(shape, index_map):
    return pl.BlockSpec(shape, index_map, pipeline_mode=pl.Buffered(1))


def _mix_out(a1, a2, w1, w2, x, gain):
    m, d = x.shape
    k1, k2 = a1.shape[1], a2.shape[1]
    tm = min(MIX_TM, m)
    return pl.pallas_call(
        _mix_out_kernel,
        grid=(m // tm,),
        in_specs=[
            pl.BlockSpec((tm, k1), lambda i: (i, 0)),
            pl.BlockSpec((tm, k2), lambda i: (i, 0)),
            _resident((k1, d), lambda i: (0, 0)),
            _resident((k2, d), lambda i: (0, 0)),
            pl.BlockSpec((tm, d), lambda i: (i, 0)),
            pl.BlockSpec((1, d), lambda i: (0, 0)),
        ],
        out_specs=pl.BlockSpec((tm, d), lambda i: (i, 0)),
        out_shape=jax.ShapeDtypeStruct((m, d), F32),
        compiler_params=_params("parallel"),
        name="mix_out",
    )(a1, a2, w1, w2, x, gain)


def _ffn_kernel(x_ref, gpre_ref, gpost_ref, up_ref, down_ref, o_ref, h_ref, acc_ref):
    f = pl.program_id(1)
    last = pl.num_programs(1) - 1

    def contribution(h):
        a = jnp.maximum(jnp.dot(h, up_ref[...], preferred_element_type=F32), 0.0)
        return jnp.dot((a * a).astype(BF16), down_ref[...], preferred_element_type=F32)

    @pl.when(f == 0)
    def _():
        for rows in _row_chunks(x_ref.shape[0]):
            h = _rms(x_ref[rows, :], gpre_ref[...]).astype(BF16)
            h_ref[rows, :] = h
            acc_ref[rows, :] = contribution(h)

    @pl.when((f > 0) & (f < last))
    def _():
        acc_ref[...] += contribution(h_ref[...])

    @pl.when(f == last)
    def _():
        for rows in _row_chunks(x_ref.shape[0]):
            y = acc_ref[rows, :] + contribution(h_ref[rows, :])
            o_ref[rows, :] = x_ref[rows, :] + _rms(y, gpost_ref[...])


def _ffn(x, gpre, gpost, up, down, layer):
    m, d = x.shape
    dff = up.shape[2]
    tm = min(FFN_TM, m)
    tf = min(FFN_TF, dff)
    assert dff // tf >= 2
    return pl.pallas_call(
        _ffn_kernel,
        grid=(m // tm, dff // tf),
        in_specs=[
            pl.BlockSpec((tm, d), lambda i, f: (i, 0)),
            pl.BlockSpec((1, d), lambda i, f: (0, 0)),
            pl.BlockSpec((1, d), lambda i, f: (0, 0)),
            pl.BlockSpec((None, d, tf), lambda i, f: (layer, 0, f)),
            pl.BlockSpec((None, tf, d), lambda i, f: (layer, f, 0)),
        ],
        out_specs=pl.BlockSpec((tm, d), lambda i, f: (i, 0)),
        out_shape=jax.ShapeDtypeStruct((m, d), F32),
        scratch_shapes=[pltpu.VMEM((tm, d), BF16), pltpu.VMEM((tm, d), F32)],
        compiler_params=_params("parallel", "arbitrary"),
        name="ffn",
    )(x, gpre, gpost, up, down)


def _ple_kernel(x_ref, gain_ref, wg_ref, p_ref, wp_ref, o_ref):
    for rows in _row_chunks(x_ref.shape[0]):
        x = x_ref[rows, :]
        gate = _sigmoid(_dot(_rms(x, gain_ref[...]), wg_ref[...]))
        o_ref[rows, :] = x + _dot(p_ref[rows, :], wp_ref[...]) * gate


def _ple(x, gain, wg, p, wp, layer):
    m, d = x.shape
    pd = p.shape[2]
    tm = min(PLE_TM, m)
    return pl.pallas_call(
        _ple_kernel,
        grid=(m // tm,),
        in_specs=[
            pl.BlockSpec((tm, d), lambda i: (i, 0)),
            pl.BlockSpec((1, d), lambda i: (0, 0)),
            _resident((None, d, d), lambda i: (layer, 0, 0)),
            pl.BlockSpec((None, tm, pd), lambda i: (layer, i, 0)),
            _resident((None, pd, d), lambda i: (layer, 0, 0)),
        ],
        out_specs=pl.BlockSpec((tm, d), lambda i: (i, 0)),
        out_shape=jax.ShapeDtypeStruct((m, d), F32),
        compiler_params=_params("parallel"),
        name="ple",
    )(x, gain, wg, p, wp)


def _pool_kernel(u_ref, prev_ref, w_ref, scale_ref, o_ref, ext_ref):
    i = pl.program_id(1)
    ts = u_ref.shape[0]
    ext_ref[pl.ds(POOL_HALO, ts), :] = u_ref[...]

    @pl.when(i == 0)
    def _():
        ext_ref[pl.ds(0, POOL_HALO), :] = jnp.zeros((POOL_HALO, POOL_WIDTH), F32)

    @pl.when(i > 0)
    def _():
        ext_ref[pl.ds(0, POOL_HALO), :] = prev_ref[...]

    t = i * ts + lax.broadcasted_iota(jnp.int32, (ts, 1), 0)
    for gi, w in enumerate(POOL_WINDOWS):
        lo = gi * POOL_GROUP
        cur = ext_ref[pl.ds(POOL_HALO, ts), pl.ds(lo, POOL_GROUP)]
        tot = cur
        for j in range(1, w):
            tot = tot + ext_ref[pl.ds(POOL_HALO - j, ts), pl.ds(lo, POOL_GROUP)]
        cnt = jnp.minimum(t + 1, w).astype(F32)
        pooled = tot / cnt - cur
        out = _dot(pooled, w_ref[gi]) * scale_ref[:, pl.ds(lo, POOL_GROUP)]
        o_ref[:, pl.ds(lo, POOL_GROUP)] = out.astype(o_ref.dtype)


def _pool(u, pool_w, pool_scale, batch):
    m, width = u.shape
    s = m // batch
    ts = min(POOL_TS, s)
    nt = s // ts
    hb = ts // POOL_HALO
    return pl.pallas_call(
        _pool_kernel,
        grid=(batch, nt),
        in_specs=[
            pl.BlockSpec((ts, width), lambda b, i: (b * nt + i, 0)),
            pl.BlockSpec((POOL_HALO, width), lambda b, i: (jnp.maximum((b * nt + i) * hb - 1, 0), 0)),
            pl.BlockSpec((len(POOL_WINDOWS), POOL_GROUP, POOL_GROUP), lambda b, i: (0, 0, 0)),
            pl.BlockSpec((1, width), lambda b, i: (0, 0)),
        ],
        out_specs=pl.BlockSpec((ts, width), lambda b, i: (b * nt + i, 0)),
        out_shape=jax.ShapeDtypeStruct((m, width), MIX_DTYPE),
        scratch_shapes=[pltpu.VMEM((POOL_HALO + ts, width), F32)],
        compiler_params=_params("parallel", "arbitrary"),
        name="pool_mixer",
    )(u, u, pool_w, pool_scale)


def _gla_constants(c):
    nlev = int(math.log2(c))
    i = np.arange(c)[:, None]
    j = np.arange(c)[None, :]
    x = i ^ j
    level = np.full((c, c), -1, np.int32)
    for lev in range(nlev):
        level = np.where((x >> lev) == 1, lev, level)
    level = np.where(j > i, -1, level)
    level = np.where(i == j, nlev, level)
    return jnp.asarray(j <= i, BF16), jnp.asarray(level, jnp.int32)


def _gla_kernel(z_ref, glr_ref, w2_ref, gb_ref, gn_ref, tri_ref, lev_ref, o_ref, st_ref, cum_ref):
    c = GLA_CHUNK
    nlev = int(math.log2(c))
    ts = z_ref.shape[0]
    sub = 8
    kq, kk, kv, kg = 0, GLA_DK, 2 * GLA_DK, 2 * GLA_DK + GLA_DV

    @pl.when(pl.program_id(2) == 0)
    def _():
        st_ref[...] = jnp.zeros_like(st_ref)

    tri = tri_ref[...]
    level = lev_ref[...]
    gn = gn_ref[...]
    row8 = lax.broadcasted_iota(jnp.int32, (sub, GLA_DK), 0)

    g_all = _log_sigmoid(_dot3(glr_ref[...], w2_ref[...]) + gb_ref[...]) * (LOG2E / GLA_TAU)
    for ci in range(ts // c):
        cum_ref[pl.ds(ci * c, c), :] = _dot_sel(tri, g_all[ci * c:(ci + 1) * c])

    def cum_row(r):
        return jnp.broadcast_to(cum_ref[pl.ds(r, 1), :], (sub, GLA_DK))

    def level_exponent(ci, lev):
        s = 1 << lev
        tiles = []
        for a in range(c // sub):
            r0 = ci * c + a * sub
            b = cum_ref[pl.ds(r0, sub), :]
            if 2 * s >= sub:
                bm = cum_row(ci * c + (a * sub // (2 * s)) * (2 * s) + s - 1)
            else:
                bm = cum_row(r0 + s - 1)
                for blk in range(2 * s, sub, 2 * s):
                    bm = jnp.where(row8 >= blk, cum_row(r0 + blk + s - 1), bm)
            tiles.append(-jnp.abs(b - bm))
        return jnp.concatenate(tiles, axis=0)

    st = st_ref[...]
    for gi in range(ts // (c * GLA_GROUP)):
        ids = range(gi * GLA_GROUP, (gi + 1) * GLA_GROUP)
        rows = {ci: pl.ds(ci * c, c) for ci in ids}
        qa = {ci: z_ref[rows[ci], kq:kq + GLA_DK] for ci in ids}
        ka = {ci: z_ref[rows[ci], kk:kk + GLA_DK] for ci in ids}
        b = {ci: cum_ref[rows[ci], :] for ci in ids}
        b_last = {ci: cum_ref[pl.ds(ci * c + c - 1, 1), :] for ci in ids}
        att = {ci: jnp.where(level == nlev, _dot_nt(qa[ci], ka[ci]), 0.0) for ci in ids}
        for lev in range(nlev):
            el = {ci: jnp.exp2(level_exponent(ci, lev)).astype(qa[ci].dtype) for ci in ids}
            att = {ci: jnp.where(level == lev, _dot_nt(qa[ci] * el[ci], ka[ci] * el[ci]), att[ci]) for ci in ids}
        intra = {ci: _dot(att[ci] * (GLA_DK ** -0.5), z_ref[rows[ci], kv:kv + GLA_DV]) for ci in ids}
        qe = {ci: qa[ci].astype(F32) * (jnp.exp2(b[ci]) * (GLA_DK ** -0.5)) for ci in ids}
        ke = {ci: ka[ci].astype(F32) * jnp.exp2(b_last[ci] - b[ci]) for ci in ids}
        for ci in ids:
            o = _dot_nt(qe[ci], st) + intra[ci]
            st = st * jnp.exp2(b_last[ci]) + _dot_tn(z_ref[rows[ci], kv:kv + GLA_DV], ke[ci])
            o = o * lax.rsqrt(jnp.mean(o * o, axis=-1, keepdims=True) + NORM_EPS) * gn
            gate = z_ref[rows[ci], kg:kg + GLA_DV].astype(F32)
            o_ref[rows[ci], :] = (o * (gate * _sigmoid(gate))).astype(o_ref.dtype)
    st_ref[...] = st


def _gla(z, glr, w2p, gate_b, gla_norm, batch):
    h, m, zw = z.shape
    s = m // batch
    ts = min(GLA_TS, s)
    nt = s // ts
    tri, level = _gla_constants(GLA_CHUNK)
    row = lambda b, hh, i: b * nt + i
    return pl.pallas_call(
        _gla_kernel,
        grid=(batch, h, nt),
        in_specs=[
            pl.BlockSpec((None, ts, zw), lambda b, hh, i: (hh, row(b, hh, i), 0)),
            pl.BlockSpec((ts, LANES), lambda b, hh, i: (row(b, hh, i), 0)),
            pl.BlockSpec((None, LANES, GLA_DK), lambda b, hh, i: (hh, 0, 0)),
            pl.BlockSpec((None, 1, GLA_DK), lambda b, hh, i: (hh, 0, 0)),
            pl.BlockSpec((1, GLA_DV), lambda b, hh, i: (0, 0)),
            pl.BlockSpec(tri.shape, lambda b, hh, i: (0, 0)),
            pl.BlockSpec(level.shape, lambda b, hh, i: (0, 0)),
        ],
        out_specs=pl.BlockSpec((ts, GLA_DV), lambda b, hh, i: (row(b, hh, i), hh)),
        out_shape=jax.ShapeDtypeStruct((m, GLA_WIDTH), MIX_DTYPE),
        scratch_shapes=[pltpu.VMEM((GLA_DV, GLA_DK), F32), pltpu.VMEM((ts, GLA_DK), F32)],
        compiler_params=_params("parallel", "parallel", "arbitrary"),
        name="gla",
    )(z, glr, w2p, gate_b, gla_norm, tri, level)


def _head_sum_matrix():
    lane = np.arange(LANES)
    return jnp.asarray((lane[:, None] // RWKV_HEAD) == (lane[None, :] // RWKV_HEAD), BF16)


def _shifted(cur, prev_rows, first):
    last = prev_rows.shape[0] - 1
    prev_row = jnp.where(first, 0.0, prev_rows[last:last + 1, :])
    rolled = pltpu.roll(cur, 1, 0)
    row = lax.broadcasted_iota(jnp.int32, cur.shape, 0)
    return jnp.where(row == 0, prev_row, rolled)


def _rwkv_prepare(first, rkv_ref, rkvp_ref, lr_ref, lrp_ref, mu_ref, mulr_ref, w0_ref, w2_ref, a0_ref,
                  a2_ref, g2_ref, kk_ref, ka_ref, hs_ref,
                  r_out, lw_out, k_out, v_out, a_out, b_out, g_out):
    def mixed(cur, prev_rows, mu):
        cur, prev_rows = cur.astype(F32), prev_rows.astype(F32)
        return cur + (_shifted(cur, prev_rows, first) - cur) * mu

    r = mixed(rkv_ref[0], rkvp_ref[0], mu_ref[0])
    k = mixed(rkv_ref[1], rkvp_ref[1], mu_ref[1])
    v = mixed(rkv_ref[2], rkvp_ref[2], mu_ref[2])
    lr = mixed(lr_ref[...], lrp_ref[...], mulr_ref[...])

    lr_wa = lr[:, 0:LANES]
    lr_g = lr[:, LANES:RWKV_LR_PAD]
    w_log = _log_sigmoid(w0_ref[...] + _dot(jnp.tanh(lr_wa), w2_ref[...])) - 0.5
    lw = -jnp.exp(w_log) * LOG2E
    a = _sigmoid(a0_ref[...] + _dot(lr_wa, a2_ref[...]))
    g = _dot(_sigmoid(lr_g), g2_ref[...])

    kk = k * kk_ref[...]
    norm = jnp.sqrt(_dot(kk * kk, hs_ref[...]))
    kk = kk / jnp.maximum(norm, 1e-12)
    r_out[...] = r
    lw_out[...] = lw
    k_out[...] = k * (1.0 + (a - 1.0) * ka_ref[...])
    v_out[...] = v
    a_out[...] = -kk
    b_out[...] = kk * a
    g_out[...] = g


def _rwkv_constants(c):
    i = np.arange(c)[:, None]
    j = np.arange(c)[None, :]
    tri = jnp.asarray(j <= i, BF16)
    i2 = np.arange(2 * c)[:, None] % c
    j2 = np.arange(2 * c)[None, :] % c
    strict = jnp.asarray(i2 > j2, F32)
    incl = jnp.asarray(i2 >= j2, F32)
    return tri, strict, incl


def _rwkv_kernel(rkv_ref, rkvp_ref, lr_ref, lrp_ref, mu_ref, mulr_ref, w0_ref, w2_ref, a0_ref, a2_ref, g2_ref,
                 kk_ref, ka_ref, lnw_ref, lnb_ref, rk_ref, tri_ref, strict_ref, incl_ref, hs_ref, o_ref,
                 state_ref, p_ref, gm_ref, rq_ref, y0_ref, r_ref, lw_ref, k_ref, v_ref, a_ref, b_ref, g_ref, y_ref):
    c = RWKV_CHUNK
    nlev = int(math.log2(c))
    ts = o_ref.shape[0]
    nchunk = ts // c
    first = pl.program_id(2) == 0

    @pl.when(first)
    def _():
        state_ref[...] = jnp.zeros_like(state_ref)

    _rwkv_prepare(first, rkv_ref, rkvp_ref, lr_ref, lrp_ref, mu_ref, mulr_ref, w0_ref, w2_ref, a0_ref, a2_ref,
                  g2_ref, kk_ref, ka_ref, hs_ref, r_ref, lw_ref, k_ref, v_ref, a_ref, b_ref, g_ref)

    lane = lax.broadcasted_iota(jnp.int32, (1, LANES), 1)
    head0 = lane < RWKV_HEAD
    tri = tri_ref[...]
    strict = strict_ref[...]
    incl = incl_ref[...]
    eye = (lax.broadcasted_iota(jnp.int32, (LANES, LANES), 0)
           == lax.broadcasted_iota(jnp.int32, (LANES, LANES), 1)).astype(F32)

    def stack(x):
        return jnp.concatenate([jnp.where(head0, x, 0.0), jnp.where(head0, 0.0, x)], axis=0)

    grp = RWKV_GROUP
    each = range(grp)

    def local(gi):
        base = gi * grp
        rows = [pl.ds((base + t) * c, c) for t in each]
        lw = [lw_ref[rows[t], :] for t in each]
        cum = [_dot_sel(tri, lw[t]) for t in each]
        last = [cum[t][c - 1:c] for t in each]
        grow = [jnp.exp2(-cum[t]) for t in each]
        fall = [jnp.exp2(last[t] - cum[t]) for t in each]
        at2 = [stack(a_ref[rows[t], :] * jnp.exp2(cum[t] - lw[t])) for t in each]
        rt2 = [stack(r_ref[rows[t], :] * jnp.exp2(cum[t])) for t in each]
        bt2 = [stack(b_ref[rows[t], :] * grow[t]) for t in each]
        kt2 = [stack(k_ref[rows[t], :] * grow[t]) for t in each]
        bh2 = [stack(b_ref[rows[t], :] * fall[t]) for t in each]
        kh2 = [stack(k_ref[rows[t], :] * fall[t]) for t in each]
        v2 = [stack(v_ref[rows[t], :]) for t in each]
        big = [_dot_nt(jnp.concatenate([at2[t], rt2[t]], axis=0), jnp.concatenate([bt2[t], kt2[t]], axis=0))
               for t in each]
        a_ab = [big[t][0:2 * c, 0:2 * c] * strict for t in each]
        a_ak = [big[t][0:2 * c, 2 * c:4 * c] * strict for t in each]
        a_rb = [big[t][2 * c:4 * c, 0:2 * c] * incl for t in each]
        a_rk = [big[t][2 * c:4 * c, 2 * c:4 * c] * incl for t in each]
        rhs = [jnp.concatenate([at2[t], _dot(a_ak[t], v2[t])], axis=1) for t in each]
        n = a_ab
        for lev in range(nlev):
            rhs = [rhs[t] + _dot(n[t], rhs[t]) for t in each]
            if lev + 1 < nlev:
                n = [_dot(n[t], n[t]) for t in each]
        for t in each:
            w2 = rhs[t][:, 0:LANES]
            z2 = rhs[t][:, LANES:2 * LANES]
            gamma = jnp.exp2(last[t])
            p_ref[base + t] = eye * gamma + _dot_tn(w2, bh2[t])
            gm_ref[base + t] = _dot_tn(jnp.concatenate([z2, v2[t]], axis=0),
                                       jnp.concatenate([bh2[t], kh2[t]], axis=0))
            rq_ref[base + t] = rt2[t] + _dot(a_rb[t], w2)
            y0_ref[base + t] = _dot(a_rb[t], z2) + _dot(a_rk[t], v2[t])

    st = state_ref[...]
    for gi in range(nchunk // grp):
        local(gi)
        for ci in range(gi * grp, (gi + 1) * grp):
            y2 = _dot_nt(rq_ref[ci], st) + y0_ref[ci]
            y_ref[pl.ds(ci * c, c), :] = y2[0:c] + y2[c:2 * c]
            st = _dot3(st, p_ref[ci]) + gm_ref[ci]
    state_ref[...] = st

    hs = hs_ref[...]
    y = y_ref[...]
    mean = _dot(y, hs) * (1.0 / RWKV_HEAD)
    yc = y - mean
    var = _dot(yc * yc, hs) * (1.0 / RWKV_HEAD)
    yn = yc * lax.rsqrt(var + RWKV_LN_EPS) * lnw_ref[...] + lnb_ref[...]
    bonus = _dot(r_ref[...] * k_ref[...] * rk_ref[...], hs) * v_ref[...]
    o_ref[...] = ((yn + bonus) * g_ref[...]).astype(o_ref.dtype)


def _rwkv(z6, lr, mu_rkv, mu_lr, w0, w2p, a0, a2p, g2p, k_k, k_a, ln_w, ln_b, r_k, batch):
    m = lr.shape[0]
    s = m // batch
    ts = min(RWKV_TS, s)
    nt = s // ts
    nb = RWKV_WIDTH // LANES
    c = RWKV_CHUNK
    nchunk = ts // c
    tri, strict, incl = _rwkv_constants(c)
    hs = _head_sum_matrix()
    row = lambda bb, j, i: bb * nt + i
    prow = lambda bb, j, i: jnp.maximum((bb * nt + i) * (ts // 8) - 1, 0)
    vecspec = pl.BlockSpec((1, LANES), lambda bb, j, i: (0, j))
    lrw = pl.BlockSpec((LANES, LANES), lambda bb, j, i: (0, j))
    lrg = pl.BlockSpec((RWKV_LR_PAD - LANES, LANES), lambda bb, j, i: (0, j))
    const = lambda arr: pl.BlockSpec(arr.shape, lambda bb, j, i: (0,) * arr.ndim)
    tile = pltpu.VMEM((ts, LANES), F32)
    return pl.pallas_call(
        _rwkv_kernel,
        grid=(batch, nb, nt),
        in_specs=[
            pl.BlockSpec((3, ts, LANES), lambda bb, j, i: (0, row(bb, j, i), j)),
            pl.BlockSpec((3, BF16_ROWS, LANES),
                         lambda bb, j, i: (0, jnp.maximum((bb * nt + i) * (ts // BF16_ROWS) - 1, 0), j)),
            pl.BlockSpec((ts, RWKV_LR_PAD), lambda bb, j, i: (row(bb, j, i), 0)),
            pl.BlockSpec((8, RWKV_LR_PAD), lambda bb, j, i: (prow(bb, j, i), 0)),
            pl.BlockSpec((3, 1, LANES), lambda bb, j, i: (0, 0, j)),
            pl.BlockSpec((1, RWKV_LR_PAD), lambda bb, j, i: (0, 0)),
            vecspec, lrw, vecspec, lrw, lrg, vecspec, vecspec,
            vecspec, vecspec, vecspec,
            const(tri), const(strict), const(incl), const(hs),
        ],
        out_specs=pl.BlockSpec((ts, LANES), lambda bb, j, i: (row(bb, j, i), j)),
        out_shape=jax.ShapeDtypeStruct((m, RWKV_WIDTH), MIX_DTYPE),
        scratch_shapes=[
            pltpu.VMEM((LANES, LANES), F32),
            pltpu.VMEM((nchunk, LANES, LANES), F32),
            pltpu.VMEM((nchunk, LANES, LANES), F32),
            pltpu.VMEM((nchunk, 2 * c, LANES), F32),
            pltpu.VMEM((nchunk, 2 * c, LANES), F32),
        ] + [tile] * 8,
        compiler_params=_params("parallel", "parallel", "arbitrary"),
        name="rwkv",
    )(z6, z6, lr, lr, mu_rkv, mu_lr, w0, w2p, a0, a2p, g2p, k_k, k_a, ln_w, ln_b, r_k, tri, strict, incl, hs)


def _rope(x, cos2, sin2):
    return x * cos2 + pltpu.roll(x, DIL_HEAD // 2, 1) * sin2


def _dil_kernel(q_ref, kp_ref, kc_ref, vp_ref, vc_ref, cos_ref, sin_ref, cosp_ref, sinp_ref, o_ref,
                qs_ref, ks_ref, vs_ref, acc_ref, m_ref, l_ref):
    ts = q_ref.shape[0]
    i = pl.program_id(2)
    blk = DIL_BLOCK
    qs_ref[...] = _rope(q_ref[...].astype(F32), cos_ref[...], sin_ref[...]) * (DIL_HEAD ** -0.5)
    ks_ref[pl.ds(0, ts), :] = _rope(kp_ref[...].astype(F32), cosp_ref[...], sinp_ref[...])
    ks_ref[pl.ds(ts, ts), :] = _rope(kc_ref[...].astype(F32), cos_ref[...], sin_ref[...])
    vs_ref[pl.ds(0, ts), :] = vp_ref[...].astype(F32)
    vs_ref[pl.ds(ts, ts), :] = vc_ref[...].astype(F32)

    qi = lax.broadcasted_iota(jnp.int32, (blk, 2 * blk), 0)
    ki = lax.broadcasted_iota(jnp.int32, (blk, 2 * blk), 1)
    dist = qi + blk - ki
    band = (dist >= 0) & (dist <= blk)

    order = sorted(range(len(DIL_PATTERNS)), key=lambda pi: -DIL_PATTERNS[pi][1])
    stored = order[:-1]
    assert DIL_PATTERNS[order[-1]][1] == 1

    for pi in order:
        window, dil = DIL_PATTERNS[pi]
        assert window // dil == blk
        per_res = ts // (blk * dil)

        def unit(u, carry, pi=pi, dil=dil, per_res=per_res):
            res = u // per_res
            nblk = u % per_res
            qstart = res + nblk * (blk * dil)
            kstart = ts + qstart - blk * dil
            q = qs_ref[pl.ds(qstart, blk, stride=dil), :]
            k = ks_ref[pl.ds(kstart, 2 * blk, stride=dil), :]
            v = vs_ref[pl.ds(kstart, 2 * blk, stride=dil), :]
            s = _dot_nt(q, k)
            key_pos = i * ts + (qstart - blk * dil) + ki * dil
            s = jnp.where(band & (key_pos >= 0), s, -jnp.inf)
            mx = jnp.max(s, axis=-1, keepdims=True)
            p = jnp.exp(s - mx)
            acc = _dot(p, v)
            mxb = jnp.broadcast_to(mx, (blk, LANES))
            lb = jnp.broadcast_to(jnp.sum(p, axis=-1, keepdims=True), (blk, LANES))
            if pi in stored:
                slot = stored.index(pi)
                acc_ref[slot, pl.ds(qstart, blk, stride=dil), :] = acc
                m_ref[slot, pl.ds(qstart, blk, stride=dil), :] = mxb
                l_ref[slot, pl.ds(qstart, blk, stride=dil), :] = lb
            else:
                rows = pl.ds(pl.multiple_of(qstart, blk), blk)
                m_all = mxb
                for slot in range(len(stored)):
                    m_all = jnp.maximum(m_all, m_ref[slot, rows, :])
                cf = jnp.exp(mxb - m_all)
                num = cf * acc
                den = cf * lb
                for slot in range(len(stored)):
                    cf = jnp.exp(m_ref[slot, rows, :] - m_all)
                    num = num + cf * acc_ref[slot, rows, :]
                    den = den + cf * l_ref[slot, rows, :]
                o_ref[rows, :] = (num / den).astype(o_ref.dtype)
            return carry

        lax.fori_loop(0, ts // blk, unit, 0, unroll=8)


def _dilated(z6, cos2, sin2, batch):
    m = z6.shape[1]
    s = m // batch
    ts = DIL_TS
    assert s % ts == 0
    nt = s // ts
    cur = lambda which: pl.BlockSpec((None, ts, DIL_HEAD), lambda b, h, i: (which, b * nt + i, h))
    prev = lambda which: pl.BlockSpec((None, ts, DIL_HEAD),
                                      lambda b, h, i: (which, b * nt + jnp.maximum(i - 1, 0), h))
    tab = pl.BlockSpec((ts, DIL_HEAD), lambda b, h, i: (i, 0))
    tabp = pl.BlockSpec((ts, DIL_HEAD), lambda b, h, i: (jnp.maximum(i - 1, 0), 0))
    npat = len(DIL_PATTERNS) - 1
    return pl.pallas_call(
        _dil_kernel,
        grid=(batch, DIL_HEADS, nt),
        in_specs=[cur(3), prev(4), cur(4), prev(5), cur(5), tab, tab, tabp, tabp],
        out_specs=pl.BlockSpec((ts, DIL_HEAD), lambda b, h, i: (b * nt + i, h)),
        out_shape=jax.ShapeDtypeStruct((m, DIL_WIDTH), MIX_DTYPE),
        scratch_shapes=[
            pltpu.VMEM((ts, DIL_HEAD), F32),
            pltpu.VMEM((2 * ts, DIL_HEAD), F32),
            pltpu.VMEM((2 * ts, DIL_HEAD), F32),
            pltpu.VMEM((npat, ts, DIL_HEAD), F32),
            pltpu.VMEM((npat, ts, LANES), F32),
            pltpu.VMEM((npat, ts, LANES), F32),
        ],
        compiler_params=_params("parallel", "parallel", "arbitrary"),
        name="dilated_attention",
    )(z6, z6, z6, z6, z6, cos2, sin2, cos2, sin2)


def _heads(w, n_heads, width):
    d = w.shape[0]
    return w.reshape(d, n_heads, width).transpose(1, 0, 2)


def _pad_rows(w, start, total):
    return jnp.zeros((total, w.shape[1]), w.dtype).at[start:start + w.shape[0]].set(w)


def _even_mixer(x, gain_pre, gain_post, w_in, w_out, pool_w, pool_scale, gate_w2, gate_b, gla_norm, batch):
    d = x.shape[1]
    o0 = POOL_WIDTH
    o1 = o0 + GLA_QK
    o2 = o1 + GLA_QK
    o3 = o2 + GLA_WIDTH
    o4 = o3 + GLA_WIDTH
    wb = w_in.astype(BF16)
    w_u = wb[:, :o0][None]
    w_z = jnp.concatenate([_heads(wb[:, o0:o1], GLA_HEADS, GLA_DK), _heads(wb[:, o1:o2], GLA_HEADS, GLA_DK),
                           _heads(wb[:, o2:o3], GLA_HEADS, GLA_DV), _heads(wb[:, o3:o4], GLA_HEADS, GLA_DV)], 2)
    w_glr = jnp.zeros((d, LANES), BF16).at[:, :GLA_GATE_RANK].set(wb[:, o4:])[None]
    u, glr, z = _proj(x, gain_pre, [w_u, w_glr, w_z], [F32, F32, ACT_DTYPE])
    u, glr = u[0], glr[0]
    a_out = _pool(u, pool_w.astype(BF16), pool_scale[None], batch)
    w2p = _heads(_pad_rows(gate_w2, 0, LANES), GLA_HEADS, GLA_DK)
    o = _gla(z, glr, w2p, gate_b.reshape(GLA_HEADS, 1, GLA_DK), gla_norm[None], batch)
    wo = w_out.astype(BF16)
    return _mix_out(a_out, o, wo[:POOL_WIDTH], wo[POOL_WIDTH:], x, gain_post)


def _odd_mixer(x, gain_pre, gain_post, w_in, w_out, mu, w0, w2, a0, a2, g2, k_k, k_a, r_k, ln_w, ln_b,
               cos2, sin2, batch):
    d = x.shape[1]
    wb = w_in.astype(BF16)
    c3 = 3 * RWKV_WIDTH
    rwkv_in = c3 + RWKV_LR
    w6 = jnp.concatenate([_heads(wb[:, :c3], 3, RWKV_WIDTH), _heads(wb[:, rwkv_in:], 3, DIL_WIDTH)], 0)
    w_lr = jnp.zeros((d, RWKV_LR_PAD), BF16).at[:, :RWKV_LR].set(wb[:, c3:rwkv_in])[None]
    z6, lr = _proj(x, gain_pre, [w6, w_lr], [ACT_DTYPE, F32])
    lr = lr[0]
    mu_rkv = mu[:c3].reshape(3, 1, RWKV_WIDTH)
    mu_lr = jnp.zeros((1, RWKV_LR_PAD), F32).at[0, :RWKV_LR].set(mu[c3:])
    w2p = _pad_rows(w2, 0, LANES).astype(BF16)
    a2p = _pad_rows(a2, RWKV_DECAY_RANK, LANES).astype(BF16)
    g2p = _pad_rows(g2, 0, RWKV_LR_PAD - LANES).astype(BF16)
    c_out = _rwkv(z6, lr, mu_rkv, mu_lr, w0[None], w2p, a0[None], a2p, g2p, k_k[None], k_a[None],
                  ln_w[None], ln_b[None], r_k.reshape(1, RWKV_WIDTH), batch)
    d_out = _dilated(z6, cos2, sin2, batch)
    wo = w_out.astype(BF16)
    return _mix_out(c_out, d_out, wo[:RWKV_WIDTH], wo[RWKV_WIDTH:], x, gain_post)


def _rope_tables(s):
    half = DIL_HEAD // 2
    inv = ROPE_THETA ** (-jnp.arange(half, dtype=F32) / half)
    ang = jnp.arange(s).astype(F32)[:, None] * inv[None, :]
    cos, sin = jnp.cos(ang), jnp.sin(ang)
    return jnp.concatenate([cos, cos], axis=-1), jnp.concatenate([-sin, sin], axis=-1)


def kernel(x, p, norm_mix_pre, norm_mix_post, norm_ffn_pre, norm_ffn_post, ev_w_in, ev_w_out, pool_w, pool_scale, gla_gate_w2, gla_gate_b, gla_norm, od_w_in, od_w_out, rwkv_mu, rwkv_w0, rwkv_w2, rwkv_a0, rwkv_a2, rwkv_g2, rwkv_k_k, rwkv_k_a, rwkv_r_k, rwkv_ln_w, rwkv_ln_b, ffn_up, ffn_down, ple_proj, ple_gate, ple_norm):
    batch, s, d = x.shape
    depth = p.shape[0]
    m = batch * s
    xf = x.reshape(m, d)
    cos2, sin2 = _rope_tables(s)
    up_all, down_all = ffn_up.astype(BF16), ffn_down.astype(BF16)
    gate_all, proj_all = ple_gate.astype(BF16), ple_proj.astype(BF16)
    p_all = p.reshape(depth, m, PLE_DIM)
    for i in range(depth):
        j = i // 2
        if i % 2 == 0:
            xf = _even_mixer(xf, norm_mix_pre[i][None], norm_mix_post[i][None], ev_w_in[j], ev_w_out[j],
                             pool_w[j], pool_scale[j], gla_gate_w2[j], gla_gate_b[j], gla_norm[j], batch)
        else:
            xf = _odd_mixer(xf, norm_mix_pre[i][None], norm_mix_post[i][None], od_w_in[j], od_w_out[j],
                            rwkv_mu[j], rwkv_w0[j], rwkv_w2[j], rwkv_a0[j], rwkv_a2[j], rwkv_g2[j],
                            rwkv_k_k[j], rwkv_k_a[j], rwkv_r_k[j], rwkv_ln_w[j], rwkv_ln_b[j],
                            cos2, sin2, batch)
        xf = _ffn(xf, norm_ffn_pre[i][None], norm_ffn_post[i][None], up_all, down_all, i)
        xf = _ple(xf, ple_norm[i][None], gate_all, p_all, proj_all, i)
    return xf.reshape(batch, s, d)
```

```python
import functools
import math

import numpy as np
import jax
import jax.numpy as jnp
from jax import lax
from jax.experimental import pallas as pl
from jax.experimental.pallas import tpu as pltpu

F32 = jnp.float32
BF16 = jnp.bfloat16
MIX_DTYPE = BF16
ACT_DTYPE = BF16

NORM_EPS = 1e-6
LOG2E = math.log2(math.e)
LANES = 128
BF16_ROWS = 16
VMEM_LIMIT_BYTES = 56 * 1024 * 1024

D_MODEL = 2048
PLE_DIM = 256
POOL_WINDOWS = (2, 4, 8, 16)
POOL_GROUP = 128
POOL_WIDTH = 512
POOL_HALO = 16
GLA_HEADS = 4
GLA_DK = 192
GLA_DV = 384
GLA_QK = GLA_HEADS * GLA_DK
GLA_WIDTH = GLA_HEADS * GLA_DV
GLA_GATE_RANK = 16
GLA_TAU = 16.0
RWKV_HEAD = 64
RWKV_WIDTH = 1024
RWKV_DECAY_RANK = 64
RWKV_A_RANK = 64
RWKV_GATE_RANK = 160
RWKV_LR = RWKV_DECAY_RANK + RWKV_A_RANK + RWKV_GATE_RANK
RWKV_LR_PAD = 384
RWKV_LN_EPS = 64e-5
DIL_WIDTH = 1024
DIL_HEAD = 128
DIL_HEADS = 8
DIL_PATTERNS = ((128, 1), (512, 4), (2048, 16))
DIL_BLOCK = 128
ROPE_THETA = 10000.0

ROW_CHUNK = 256
PROJ_TM = 1024
MIX_TM = 1024
FFN_TM = 512
FFN_TF = 1024
PLE_TM = 1024
POOL_TS = 512
GLA_CHUNK = 64
GLA_TS = 1024
GLA_GROUP = 8
RWKV_CHUNK = 64
RWKV_TS = 1024
RWKV_GROUP = 16
DIL_TS = 2048


def _params(*sem):
    return pltpu.CompilerParams(dimension_semantics=sem, vmem_limit_bytes=VMEM_LIMIT_BYTES)


def _dot(a, b):
    return jnp.dot(a.astype(BF16), b.astype(BF16), preferred_element_type=F32)


def _dot_nt(a, b):
    return lax.dot_general(a.astype(BF16), b.astype(BF16), (((1,), (1,)), ((), ())),
                           preferred_element_type=F32)


def _dot_tn(a, b):
    return lax.dot_general(a.astype(BF16), b.astype(BF16), (((0,), (0,)), ((), ())),
                           preferred_element_type=F32)


def _split(x):
    hi = x.astype(BF16)
    return hi, (x - hi.astype(F32)).astype(BF16)


def _dot3_general(a, b, dims):
    ah, al = _split(a)
    bh, bl = _split(b)
    d = lambda u, v: lax.dot_general(u, v, (dims, ((), ())), preferred_element_type=F32)
    return d(ah, bh) + d(ah, bl) + d(al, bh)


def _dot3(a, b):
    return _dot3_general(a, b, ((1,), (0,)))


def _dot3_nt(a, b):
    return _dot3_general(a, b, ((1,), (1,)))


def _dot_sel(sel_bf16, x):
    hi = x.astype(BF16)
    r1 = x - hi.astype(F32)
    mid = r1.astype(BF16)
    lo = (r1 - mid.astype(F32)).astype(BF16)
    out = jnp.dot(sel_bf16, hi, preferred_element_type=F32)
    out = out + jnp.dot(sel_bf16, mid, preferred_element_type=F32)
    return out + jnp.dot(sel_bf16, lo, preferred_element_type=F32)


def _row_chunks(rows):
    step = min(ROW_CHUNK, rows)
    return [pl.ds(r, step) for r in range(0, rows, step)]


def _rms(x, gain):
    return x * lax.rsqrt(jnp.mean(x * x, axis=-1, keepdims=True) + NORM_EPS) * gain


def _sigmoid(x):
    return 1.0 / (1.0 + jnp.exp(-x))


def _log_sigmoid(x):
    return jnp.minimum(x, 0.0) - jnp.log1p(jnp.exp(-jnp.abs(x)))


def _proj_kernel(starts, x_ref, gain_ref, *refs):
    nw = len(starts) - 1
    w_refs, o_refs, h_ref = refs[:nw], refs[nw:2 * nw], refs[2 * nw]
    j = pl.program_id(1)

    @pl.when(j == 0)
    def _():
        for rows in _row_chunks(x_ref.shape[0]):
            h = _rms(x_ref[rows, :], gain_ref[...]).astype(BF16)
            h_ref[rows, :] = h
            o_refs[0][rows, :] = jnp.dot(h, w_refs[0][...], preferred_element_type=F32).astype(o_refs[0].dtype)

    for k in range(nw):
        @pl.when((j >= max(starts[k], 1)) & (j < starts[k + 1]))
        def _(k=k):
            o_refs[k][...] = jnp.dot(h_ref[...], w_refs[k][...],
                                     preferred_element_type=F32).astype(o_refs[k].dtype)


def _proj(x, gain, ws, dtypes):
    m, d = x.shape
    tm = min(PROJ_TM, m)
    starts = [0]
    for w in ws:
        starts.append(starts[-1] + w.shape[0])

    def group(k):
        return lambda i, j: jnp.clip(j - starts[k], 0, ws[k].shape[0] - 1)

    w_specs = [pl.BlockSpec((None, d, w.shape[2]), lambda i, j, gk=group(k): (gk(i, j), 0, 0))
               for k, w in enumerate(ws)]
    o_specs = [pl.BlockSpec((None, tm, w.shape[2]), lambda i, j, gk=group(k): (gk(i, j), i, 0))
               for k, w in enumerate(ws)]
    return pl.pallas_call(
        functools.partial(_proj_kernel, tuple(starts)),
        grid=(m // tm, starts[-1]),
        in_specs=[pl.BlockSpec((tm, d), lambda i, j: (i, 0)), pl.BlockSpec((1, d), lambda i, j: (0, 0))] + w_specs,
        out_specs=o_specs,
        out_shape=[jax.ShapeDtypeStruct((w.shape[0], m, w.shape[2]), dt) for w, dt in zip(ws, dtypes)],
        scratch_shapes=[pltpu.VMEM((tm, d), BF16)],
        compiler_params=_params("parallel", "arbitrary"),
        name="norm_proj",
    )(x, gain, *ws)


def _mix_out_kernel(a1_ref, a2_ref, w1_ref, w2_ref, x_ref, gain_ref, o_ref):
    for rows in _row_chunks(x_ref.shape[0]):
        y = _dot(a1_ref[rows, :], w1_ref[...]) + _dot(a2_ref[rows, :], w2_ref[...])
        o_ref[rows, :] = x_ref[rows, :] + _rms(y, gain_ref[...])


def _resident(shape, index_map):
    return pl.BlockSpec(shape, index_map, pipeline_mode=pl.Buffered(1))


def _mix_out(a1, a2, w1, w2, x, gain):
    m, d = x.shape
    k1, k2 = a1.shape[1], a2.shape[1]
    tm = min(MIX_TM, m)
    return pl.pallas_call(
        _mix_out_kernel,
        grid=(m // tm,),
        in_specs=[
            pl.BlockSpec((tm, k1), lambda i: (i, 0)),
            pl.BlockSpec((tm, k2), lambda i: (i, 0)),
            _resident((k1, d), lambda i: (0, 0)),
            _resident((k2, d), lambda i: (0, 0)),
            pl.BlockSpec((tm, d), lambda i: (i, 0)),
            pl.BlockSpec((1, d), lambda i: (0, 0)),
        ],
        out_specs=pl.BlockSpec((tm, d), lambda i: (i, 0)),
        out_shape=jax.ShapeDtypeStruct((m, d), F32),
        compiler_params=_params("parallel"),
        name="mix_out",
    )(a1, a2, w1, w2, x, gain)


def _ffn_kernel(x_ref, gpre_ref, gpost_ref, up_ref, down_ref, o_ref, h_ref, acc_ref):
    f = pl.program_id(1)
    last = pl.num_programs(1) - 1

    def contribution(h):
        a = jnp.maximum(jnp.dot(h, up_ref[...], preferred_element_type=F32), 0.0)
        return jnp.dot((a * a).astype(BF16), down_ref[...], preferred_element_type=F32)

    @pl.when(f == 0)
    def _():
        for rows in _row_chunks(x_ref.shape[0]):
            h = _rms(x_ref[rows, :], gpre_ref[...]).astype(BF16)
            h_ref[rows, :] = h
            acc_ref[rows, :] = contribution(h)

    @pl.when((f > 0) & (f < last))
    def _():
        acc_ref[...] += contribution(h_ref[...])

    @pl.when(f == last)
    def _():
        for rows in _row_chunks(x_ref.shape[0]):
            y = acc_ref[rows, :] + contribution(h_ref[rows, :])
            o_ref[rows, :] = x_ref[rows, :] + _rms(y, gpost_ref[...])


def _ffn(x, gpre, gpost, up, down, layer):
    m, d = x.shape
    dff = up.shape[2]
    tm = min(FFN_TM, m)
    tf = min(FFN_TF, dff)
    assert dff // tf >= 2
    return pl.pallas_call(
        _ffn_kernel,
        grid=(m // tm, dff // tf),
        in_specs=[
            pl.BlockSpec((tm, d), lambda i, f: (i, 0)),
            pl.BlockSpec((1, d), lambda i, f: (0, 0)),
            pl.BlockSpec((1, d), lambda i, f: (0, 0)),
            pl.BlockSpec((None, d, tf), lambda i, f: (layer, 0, f)),
            pl.BlockSpec((None, tf, d), lambda i, f: (layer, f, 0)),
        ],
        out_specs=pl.BlockSpec((tm, d), lambda i, f: (i, 0)),
        out_shape=jax.ShapeDtypeStruct((m, d), F32),
        scratch_shapes=[pltpu.VMEM((tm, d), BF16), pltpu.VMEM((tm, d), F32)],
        compiler_params=_params("parallel", "arbitrary"),
        name="ffn",
    )(x, gpre, gpost, up, down)


def _ple_kernel(x_ref, gain_ref, wg_ref, p_ref, wp_ref, o_ref):
    for rows in _row_chunks(x_ref.shape[0]):
        x = x_ref[rows, :]
        gate = _sigmoid(_dot(_rms(x, gain_ref[...]), wg_ref[...]))
        o_ref[rows, :] = x + _dot(p_ref[rows, :], wp_ref[...]) * gate


def _ple(x, gain, wg, p, wp, layer):
    m, d = x.shape
    pd = p.shape[2]
    tm = min(PLE_TM, m)
    return pl.pallas_call(
        _ple_kernel,
        grid=(m // tm,),
        in_specs=[
            pl.BlockSpec((tm, d), lambda i: (i, 0)),
            pl.BlockSpec((1, d), lambda i: (0, 0)),
            _resident((None, d, d), lambda i: (layer, 0, 0)),
            pl.BlockSpec((None, tm, pd), lambda i: (layer, i, 0)),
            _resident((None, pd, d), lambda i: (layer, 0, 0)),
        ],
        out_specs=pl.BlockSpec((tm, d), lambda i: (i, 0)),
        out_shape=jax.ShapeDtypeStruct((m, d), F32),
        compiler_params=_params("parallel"),
        name="ple",
    )(x, gain, wg, p, wp)


def _pool_kernel(u_ref, prev_ref, w_ref, scale_ref, o_ref, ext_ref):
    i = pl.program_id(1)
    ts = u_ref.shape[0]
    ext_ref[pl.ds(POOL_HALO, ts), :] = u_ref[...]

    @pl.when(i == 0)
    def _():
        ext_ref[pl.ds(0, POOL_HALO), :] = jnp.zeros((POOL_HALO, POOL_WIDTH), F32)

    @pl.when(i > 0)
    def _():
        ext_ref[pl.ds(0, POOL_HALO), :] = prev_ref[...]

    t = i * ts + lax.broadcasted_iota(jnp.int32, (ts, 1), 0)
    for gi, w in enumerate(POOL_WINDOWS):
        lo = gi * POOL_GROUP
        cur = ext_ref[pl.ds(POOL_HALO, ts), pl.ds(lo, POOL_GROUP)]
        tot = cur
        for j in range(1, w):
            tot = tot + ext_ref[pl.ds(POOL_HALO - j, ts), pl.ds(lo, POOL_GROUP)]
        cnt = jnp.minimum(t + 1, w).astype(F32)
        pooled = tot / cnt - cur
        out = _dot(pooled, w_ref[gi]) * scale_ref[:, pl.ds(lo, POOL_GROUP)]
        o_ref[:, pl.ds(lo, POOL_GROUP)] = out.astype(o_ref.dtype)


def _pool(u, pool_w, pool_scale, batch):
    m, width = u.shape
    s = m // batch
    ts = min(POOL_TS, s)
    nt = s // ts
    hb = ts // POOL_HALO
    return pl.pallas_call(
        _pool_kernel,
        grid=(batch, nt),
        in_specs=[
            pl.BlockSpec((ts, width), lambda b, i: (b * nt + i, 0)),
            pl.BlockSpec((POOL_HALO, width), lambda b, i: (jnp.maximum((b * nt + i) * hb - 1, 0), 0)),
            pl.BlockSpec((len(POOL_WINDOWS), POOL_GROUP, POOL_GROUP), lambda b, i: (0, 0, 0)),
            pl.BlockSpec((1, width), lambda b, i: (0, 0)),
        ],
        out_specs=pl.BlockSpec((ts, width), lambda b, i: (b * nt + i, 0)),
        out_shape=jax.ShapeDtypeStruct((m, width), MIX_DTYPE),
        scratch_shapes=[pltpu.VMEM((POOL_HALO + ts, width), F32)],
        compiler_params=_params("parallel", "arbitrary"),
        name="pool_mixer",
    )(u, u, pool_w, pool_scale)


def _gla_constants(c):
    nlev = int(math.log2(c))
    i = np.arange(c)[:, None]
    j = np.arange(c)[None, :]
    x = i ^ j
    level = np.full((c, c), -1, np.int32)
    for lev in range(nlev):
        level = np.where((x >> lev) == 1, lev, level)
    level = np.where(j > i, -1, level)
    level = np.where(i == j, nlev, level)
    return jnp.asarray(j <= i, BF16), jnp.asarray(level, jnp.int32)


def _gla_kernel(z_ref, glr_ref, w2_ref, gb_ref, gn_ref, tri_ref, lev_ref, o_ref, st_ref, cum_ref):
    c = GLA_CHUNK
    nlev = int(math.log2(c))
    ts = z_ref.shape[0]
    sub = 8
    kq, kk, kv, kg = 0, GLA_DK, 2 * GLA_DK, 2 * GLA_DK + GLA_DV

    @pl.when(pl.program_id(2) == 0)
    def _():
        st_ref[...] = jnp.zeros_like(st_ref)

    tri = tri_ref[...]
    level = lev_ref[...]
    gn = gn_ref[...]
    row8 = lax.broadcasted_iota(jnp.int32, (sub, GLA_DK), 0)

    g_all = _log_sigmoid(_dot3(glr_ref[...], w2_ref[...]) + gb_ref[...]) * (LOG2E / GLA_TAU)
    for ci in range(ts // c):
        cum_ref[pl.ds(ci * c, c), :] = _dot_sel(tri, g_all[ci * c:(ci + 1) * c])

    def cum_row(r):
        return jnp.broadcast_to(cum_ref[pl.ds(r, 1), :], (sub, GLA_DK))

    def level_exponent(ci, lev):
        s = 1 << lev
        tiles = []
        for a in range(c // sub):
            r0 = ci * c + a * sub
            b = cum_ref[pl.ds(r0, sub), :]
            if 2 * s >= sub:
                bm = cum_row(ci * c + (a * sub // (2 * s)) * (2 * s) + s - 1)
            else:
                bm = cum_row(r0 + s - 1)
                for blk in range(2 * s, sub, 2 * s):
                    bm = jnp.where(row8 >= blk, cum_row(r0 + blk + s - 1), bm)
            tiles.append(-jnp.abs(b - bm))
        return jnp.concatenate(tiles, axis=0)

    st = st_ref[...]
    for gi in range(ts // (c * GLA_GROUP)):
        ids = range(gi * GLA_GROUP, (gi + 1) * GLA_GROUP)
        rows = {ci: pl.ds(ci * c, c) for ci in ids}
        qa = {ci: z_ref[rows[ci], kq:kq + GLA_DK] for ci in ids}
        ka = {ci: z_ref[rows[ci], kk:kk + GLA_DK] for ci in ids}
        b = {ci: cum_ref[rows[ci], :] for ci in ids}
        b_last = {ci: cum_ref[pl.ds(ci * c + c - 1, 1), :] for ci in ids}
        att = {ci: jnp.where(level == nlev, _dot_nt(qa[ci], ka[ci]), 0.0) for ci in ids}
        for lev in range(nlev):
            el = {ci: jnp.exp2(level_exponent(ci, lev)).astype(qa[ci].dtype) for ci in ids}
            att = {ci: jnp.where(level == lev, _dot_nt(qa[ci] * el[ci], ka[ci] * el[ci]), att[ci]) for ci in ids}
        intra = {ci: _dot(att[ci] * (GLA_DK ** -0.5), z_ref[rows[ci], kv:kv + GLA_DV]) for ci in ids}
        qe = {ci: qa[ci] * (jnp.exp2(b[ci]) * (GLA_DK ** -0.5)).astype(qa[ci].dtype) for ci in ids}
        ke = {ci: ka[ci] * jnp.exp2(b_last[ci] - b[ci]).astype(ka[ci].dtype) for ci in ids}
        for ci in ids:
            o = _dot_nt(qe[ci], st) + intra[ci]
            st = st * jnp.exp2(b_last[ci]) + _dot_tn(z_ref[rows[ci], kv:kv + GLA_DV], ke[ci])
            o = o * lax.rsqrt(jnp.mean(o * o, axis=-1, keepdims=True) + NORM_EPS) * gn
            gate = z_ref[rows[ci], kg:kg + GLA_DV].astype(F32)
            o_ref[rows[ci], :] = (o * (gate * _sigmoid(gate))).astype(o_ref.dtype)
    st_ref[...] = st


def _gla(z, glr, w2p, gate_b, gla_norm, batch):
    h, m, zw = z.shape
    s = m // batch
    ts = min(GLA_TS, s)
    nt = s // ts
    tri, level = _gla_constants(GLA_CHUNK)
    row = lambda b, hh, i: b * nt + i
    return pl.pallas_call(
        _gla_kernel,
        grid=(batch, h, nt),
        in_specs=[
            pl.BlockSpec((None, ts, zw), lambda b, hh, i: (hh, row(b, hh, i), 0)),
            pl.BlockSpec((ts, LANES), lambda b, hh, i: (row(b, hh, i), 0)),
            pl.BlockSpec((None, LANES, GLA_DK), lambda b, hh, i: (hh, 0, 0)),
            pl.BlockSpec((None, 1, GLA_DK), lambda b, hh, i: (hh, 0, 0)),
            pl.BlockSpec((1, GLA_DV), lambda b, hh, i: (0, 0)),
            pl.BlockSpec(tri.shape, lambda b, hh, i: (0, 0)),
            pl.BlockSpec(level.shape, lambda b, hh, i: (0, 0)),
        ],
        out_specs=pl.BlockSpec((ts, GLA_DV), lambda b, hh, i: (row(b, hh, i), hh)),
        out_shape=jax.ShapeDtypeStruct((m, GLA_WIDTH), MIX_DTYPE),
        scratch_shapes=[pltpu.VMEM((GLA_DV, GLA_DK), F32), pltpu.VMEM((ts, GLA_DK), F32)],
        compiler_params=_params("parallel", "parallel", "arbitrary"),
        name="gla",
    )(z, glr, w2p, gate_b, gla_norm, tri, level)


def _head_sum_matrix():
    lane = np.arange(LANES)
    return jnp.asarray((lane[:, None] // RWKV_HEAD) == (lane[None, :] // RWKV_HEAD), BF16)


def _shifted(cur, prev_rows, first):
    last = prev_rows.shape[0] - 1
    prev_row = jnp.where(first, 0.0, prev_rows[last:last + 1, :])
    rolled = pltpu.roll(cur, 1, 0)
    row = lax.broadcasted_iota(jnp.int32, cur.shape, 0)
    return jnp.where(row == 0, prev_row, rolled)


def _rwkv_prepare(first, rkv_ref, rkvp_ref, lr_ref, lrp_ref, mu_ref, mulr_ref, w0_ref, w2_ref, a0_ref,
                  a2_ref, g2_ref, kk_ref, ka_ref, hs_ref,
                  r_out, lw_out, k_out, v_out, a_out, b_out, g_out):
    def mixed(cur, prev_rows, mu):
        cur, prev_rows = cur.astype(F32), prev_rows.astype(F32)
        return cur + (_shifted(cur, prev_rows, first) - cur) * mu

    r = mixed(rkv_ref[0], rkvp_ref[0], mu_ref[0])
    k = mixed(rkv_ref[1], rkvp_ref[1], mu_ref[1])
    v = mixed(rkv_ref[2], rkvp_ref[2], mu_ref[2])
    lr = mixed(lr_ref[...], lrp_ref[...], mulr_ref[...])

    lr_wa = lr[:, 0:LANES]
    lr_g = lr[:, LANES:RWKV_LR_PAD]
    w_log = _log_sigmoid(w0_ref[...] + _dot(jnp.tanh(lr_wa), w2_ref[...])) - 0.5
    lw = -jnp.exp(w_log) * LOG2E
    a = _sigmoid(a0_ref[...] + _dot(lr_wa, a2_ref[...]))
    g = _dot(_sigmoid(lr_g), g2_ref[...])

    kk = k * kk_ref[...]
    norm = jnp.sqrt(_dot(kk * kk, hs_ref[...]))
    kk = kk / jnp.maximum(norm, 1e-12)
    r_out[...] = r
    lw_out[...] = lw
    k_out[...] = k * (1.0 + (a - 1.0) * ka_ref[...])
    v_out[...] = v
    a_out[...] = -kk
    b_out[...] = kk * a
    g_out[...] = g


def _rwkv_constants(c):
    i = np.arange(c)[:, None]
    j = np.arange(c)[None, :]
    tri = jnp.asarray(j <= i, BF16)
    i2 = np.arange(2 * c)[:, None] % c
    j2 = np.arange(2 * c)[None, :] % c
    strict = jnp.asarray(i2 > j2, F32)
    incl = jnp.asarray(i2 >= j2, F32)
    return tri, strict, incl


def _rwkv_kernel(rkv_ref, rkvp_ref, lr_ref, lrp_ref, mu_ref, mulr_ref, w0_ref, w2_ref, a0_ref, a2_ref, g2_ref,
                 kk_ref, ka_ref, lnw_ref, lnb_ref, rk_ref, tri_ref, strict_ref, incl_ref, hs_ref, o_ref,
                 state_ref, p_ref, gm_ref, rq_ref, y0_ref, r_ref, lw_ref, k_ref, v_ref, a_ref, b_ref, g_ref, y_ref):
    c = RWKV_CHUNK
    nlev = int(math.log2(c))
    ts = o_ref.shape[0]
    nchunk = ts // c
    first = pl.program_id(2) == 0

    @pl.when(first)
    def _():
        state_ref[...] = jnp.zeros_like(state_ref)

    _rwkv_prepare(first, rkv_ref, rkvp_ref, lr_ref, lrp_ref, mu_ref, mulr_ref, w0_ref, w2_ref, a0_ref, a2_ref,
                  g2_ref, kk_ref, ka_ref, hs_ref, r_ref, lw_ref, k_ref, v_ref, a_ref, b_ref, g_ref)

    lane = lax.broadcasted_iota(jnp.int32, (1, LANES), 1)
    head0 = lane < RWKV_HEAD
    tri = tri_ref[...]
    strict = strict_ref[...]
    incl = incl_ref[...]
    eye = (lax.broadcasted_iota(jnp.int32, (LANES, LANES), 0)
           == lax.broadcasted_iota(jnp.int32, (LANES, LANES), 1)).astype(F32)

    def stack(x):
        return jnp.concatenate([jnp.where(head0, x, 0.0), jnp.where(head0, 0.0, x)], axis=0)

    grp = RWKV_GROUP
    each = range(grp)

    def local(gi):
        base = gi * grp
        rows = [pl.ds((base + t) * c, c) for t in each]
        lw = [lw_ref[rows[t], :] for t in each]
        cum = [_dot_sel(tri, lw[t]) for t in each]
        last = [cum[t][c - 1:c] for t in each]
        grow = [jnp.exp2(-cum[t]) for t in each]
        fall = [jnp.exp2(last[t] - cum[t]) for t in each]
        at2 = [stack(a_ref[rows[t], :] * jnp.exp2(cum[t] - lw[t])) for t in each]
        rt2 = [stack(r_ref[rows[t], :] * jnp.exp2(cum[t])) for t in each]
        bt2 = [stack(b_ref[rows[t], :] * grow[t]) for t in each]
        kt2 = [stack(k_ref[rows[t], :] * grow[t]) for t in each]
        bh2 = [stack(b_ref[rows[t], :] * fall[t]) for t in each]
        kh2 = [stack(k_ref[rows[t], :] * fall[t]) for t in each]
        v2 = [stack(v_ref[rows[t], :]) for t in each]
        big = [_dot_nt(jnp.concatenate([at2[t], rt2[t]], axis=0), jnp.concatenate([bt2[t], kt2[t]], axis=0))
               for t in each]
        a_ab = [big[t][0:2 * c, 0:2 * c] * strict for t in each]
        a_ak = [big[t][0:2 * c, 2 * c:4 * c] * strict for t in each]
        a_rb = [big[t][2 * c:4 * c, 0:2 * c] * incl for t in each]
        a_rk = [big[t][2 * c:4 * c, 2 * c:4 * c] * incl for t in each]
        rhs = [jnp.concatenate([at2[t], _dot(a_ak[t], v2[t])], axis=1) for t in each]
        n = a_ab
        for lev in range(nlev):
            rhs = [rhs[t] + _dot(n[t], rhs[t]) for t in each]
            if lev + 1 < nlev:
                n = [_dot(n[t], n[t]) for t in each]
        for t in each:
            w2 = rhs[t][:, 0:LANES]
            z2 = rhs[t][:, LANES:2 * LANES]
            gamma = jnp.exp2(last[t])
            p_ref[base + t] = eye * gamma + _dot_tn(w2, bh2[t])
            gm_ref[base + t] = _dot_tn(jnp.concatenate([z2, v2[t]], axis=0),
                                       jnp.concatenate([bh2[t], kh2[t]], axis=0))
            rq_ref[base + t] = rt2[t] + _dot(a_rb[t], w2)
            y0_ref[base + t] = _dot(a_rb[t], z2) + _dot(a_rk[t], v2[t])

    st = state_ref[...]
    for gi in range(nchunk // grp):
        local(gi)
        for ci in range(gi * grp, (gi + 1) * grp):
            y2 = _dot_nt(rq_ref[ci], st) + y0_ref[ci]
            y_ref[pl.ds(ci * c, c), :] = y2[0:c] + y2[c:2 * c]
            st = _dot3(st, p_ref[ci]) + gm_ref[ci]
    state_ref[...] = st

    hs = hs_ref[...]
    y = y_ref[...]
    mean = _dot(y, hs) * (1.0 / RWKV_HEAD)
    yc = y - mean
    var = _dot(yc * yc, hs) * (1.0 / RWKV_HEAD)
    yn = yc * lax.rsqrt(var + RWKV_LN_EPS) * lnw_ref[...] + lnb_ref[...]
    bonus = _dot(r_ref[...] * k_ref[...] * rk_ref[...], hs) * v_ref[...]
    o_ref[...] = ((yn + bonus) * g_ref[...]).astype(o_ref.dtype)


def _rwkv(z6, lr, mu_rkv, mu_lr, w0, w2p, a0, a2p, g2p, k_k, k_a, ln_w, ln_b, r_k, batch):
    m = lr.shape[0]
    s = m // batch
    ts = min(RWKV_TS, s)
    nt = s // ts
    nb = RWKV_WIDTH // LANES
    c = RWKV_CHUNK
    nchunk = ts // c
    tri, strict, incl = _rwkv_constants(c)
    hs = _head_sum_matrix()
    row = lambda bb, j, i: bb * nt + i
    prow = lambda bb, j, i: jnp.maximum((bb * nt + i) * (ts // 8) - 1, 0)
    vecspec = pl.BlockSpec((1, LANES), lambda bb, j, i: (0, j))
    lrw = pl.BlockSpec((LANES, LANES), lambda bb, j, i: (0, j))
    lrg = pl.BlockSpec((RWKV_LR_PAD - LANES, LANES), lambda bb, j, i: (0, j))
    const = lambda arr: pl.BlockSpec(arr.shape, lambda bb, j, i: (0,) * arr.ndim)
    tile = pltpu.VMEM((ts, LANES), F32)
    return pl.pallas_call(
        _rwkv_kernel,
        grid=(batch, nb, nt),
        in_specs=[
            pl.BlockSpec((3, ts, LANES), lambda bb, j, i: (0, row(bb, j, i), j)),
            pl.BlockSpec((3, BF16_ROWS, LANES),
                         lambda bb, j, i: (0, jnp.maximum((bb * nt + i) * (ts // BF16_ROWS) - 1, 0), j)),
            pl.BlockSpec((ts, RWKV_LR_PAD), lambda bb, j, i: (row(bb, j, i), 0)),
            pl.BlockSpec((8, RWKV_LR_PAD), lambda bb, j, i: (prow(bb, j, i), 0)),
            pl.BlockSpec((3, 1, LANES), lambda bb, j, i: (0, 0, j)),
            pl.BlockSpec((1, RWKV_LR_PAD), lambda bb, j, i: (0, 0)),
            vecspec, lrw, vecspec, lrw, lrg, vecspec, vecspec,
            vecspec, vecspec, vecspec,
            const(tri), const(strict), const(incl), const(hs),
        ],
        out_specs=pl.BlockSpec((ts, LANES), lambda bb, j, i: (row(bb, j, i), j)),
        out_shape=jax.ShapeDtypeStruct((m, RWKV_WIDTH), MIX_DTYPE),
        scratch_shapes=[
            pltpu.VMEM((LANES, LANES), F32),
            pltpu.VMEM((nchunk, LANES, LANES), F32),
            pltpu.VMEM((nchunk, LANES, LANES), F32),
            pltpu.VMEM((nchunk, 2 * c, LANES), F32),
            pltpu.VMEM((nchunk, 2 * c, LANES), F32),
        ] + [tile] * 8,
        compiler_params=_params("parallel", "parallel", "arbitrary"),
        name="rwkv",
    )(z6, z6, lr, lr, mu_rkv, mu_lr, w0, w2p, a0, a2p, g2p, k_k, k_a, ln_w, ln_b, r_k, tri, strict, incl, hs)


def _rope(x, cos2, sin2):
    return x * cos2 + pltpu.roll(x, DIL_HEAD // 2, 1) * sin2


def _dil_kernel(q_ref, kp_ref, kc_ref, vp_ref, vc_ref, cos_ref, sin_ref, cosp_ref, sinp_ref, o_ref,
                qs_ref, ks_ref, vs_ref, acc_ref, m_ref, l_ref):
    ts = q_ref.shape[0]
    i = pl.program_id(2)
    blk = DIL_BLOCK
    qs_ref[...] = _rope(q_ref[...].astype(F32), cos_ref[...], sin_ref[...]) * (DIL_HEAD ** -0.5)
    ks_ref[pl.ds(0, ts), :] = _rope(kp_ref[...].astype(F32), cosp_ref[...], sinp_ref[...])
    ks_ref[pl.ds(ts, ts), :] = _rope(kc_ref[...].astype(F32), cos_ref[...], sin_ref[...])
    vs_ref[pl.ds(0, ts), :] = vp_ref[...].astype(F32)
    vs_ref[pl.ds(ts, ts), :] = vc_ref[...].astype(F32)

    qi = lax.broadcasted_iota(jnp.int32, (blk, 2 * blk), 0)
    ki = lax.broadcasted_iota(jnp.int32, (blk, 2 * blk), 1)
    dist = qi + blk - ki
    band = (dist >= 0) & (dist <= blk)

    order = sorted(range(len(DIL_PATTERNS)), key=lambda pi: -DIL_PATTERNS[pi][1])
    stored = order[:-1]
    assert DIL_PATTERNS[order[-1]][1] == 1

    for pi in order:
        window, dil = DIL_PATTERNS[pi]
        assert window // dil == blk
        per_res = ts // (blk * dil)

        def unit(u, carry, pi=pi, dil=dil, per_res=per_res):
            res = u // per_res
            nblk = u % per_res
            qstart = res + nblk * (blk * dil)
            kstart = ts + qstart - blk * dil
            q = qs_ref[pl.ds(qstart, blk, stride=dil), :]
            k = ks_ref[pl.ds(kstart, 2 * blk, stride=dil), :]
            v = vs_ref[pl.ds(kstart, 2 * blk, stride=dil), :]
            s = _dot_nt(q, k)
            key_pos = i * ts + (qstart - blk * dil) + ki * dil
            s = jnp.where(band & (key_pos >= 0), s, -jnp.inf)
            mx = jnp.max(s, axis=-1, keepdims=True)
            p = jnp.exp(s - mx)
            acc = _dot(p, v)
            mxb = jnp.broadcast_to(mx, (blk, LANES))
            lb = jnp.broadcast_to(jnp.sum(p, axis=-1, keepdims=True), (blk, LANES))
            if pi in stored:
                slot = stored.index(pi)
                acc_ref[slot, pl.ds(qstart, blk, stride=dil), :] = acc
                m_ref[slot, pl.ds(qstart, blk, stride=dil), :] = mxb
                l_ref[slot, pl.ds(qstart, blk, stride=dil), :] = lb
            else:
                rows = pl.ds(pl.multiple_of(qstart, blk), blk)
                m_all = mxb
                for slot in range(len(stored)):
                    m_all = jnp.maximum(m_all, m_ref[slot, rows, :])
                cf = jnp.exp(mxb - m_all)
                num = cf * acc
                den = cf * lb
                for slot in range(len(stored)):
                    cf = jnp.exp(m_ref[slot, rows, :] - m_all)
                    num = num + cf * acc_ref[slot, rows, :]
                    den = den + cf * l_ref[slot, rows, :]
                o_ref[rows, :] = (num / den).astype(o_ref.dtype)
            return carry

        lax.fori_loop(0, ts // blk, unit, 0, unroll=8)


def _dilated(z6, cos2, sin2, batch):
    m = z6.shape[1]
    s = m // batch
    ts = DIL_TS
    assert s % ts == 0
    nt = s // ts
    cur = lambda which: pl.BlockSpec((None, ts, DIL_HEAD), lambda b, h, i: (which, b * nt + i, h))
    prev = lambda which: pl.BlockSpec((None, ts, DIL_HEAD),
                                      lambda b, h, i: (which, b * nt + jnp.maximum(i - 1, 0), h))
    tab = pl.BlockSpec((ts, DIL_HEAD), lambda b, h, i: (i, 0))
    tabp = pl.BlockSpec((ts, DIL_HEAD), lambda b, h, i: (jnp.maximum(i - 1, 0), 0))
    npat = len(DIL_PATTERNS) - 1
    return pl.pallas_call(
        _dil_kernel,
        grid=(batch, DIL_HEADS, nt),
        in_specs=[cur(3), prev(4), cur(4), prev(5), cur(5), tab, tab, tabp, tabp],
        out_specs=pl.BlockSpec((ts, DIL_HEAD), lambda b, h, i: (b * nt + i, h)),
        out_shape=jax.ShapeDtypeStruct((m, DIL_WIDTH), MIX_DTYPE),
        scratch_shapes=[
            pltpu.VMEM((ts, DIL_HEAD), F32),
            pltpu.VMEM((2 * ts, DIL_HEAD), F32),
            pltpu.VMEM((2 * ts, DIL_HEAD), F32),
            pltpu.VMEM((npat, ts, DIL_HEAD), F32),
            pltpu.VMEM((npat, ts, LANES), F32),
            pltpu.VMEM((npat, ts, LANES), F32),
        ],
        compiler_params=_params("parallel", "parallel", "arbitrary"),
        name="dilated_attention",
    )(z6, z6, z6, z6, z6, cos2, sin2, cos2, sin2)


def _heads(w, n_heads, width):
    d = w.shape[0]
    return w.reshape(d, n_heads, width).transpose(1, 0, 2)


def _pad_rows(w, start, total):
    return jnp.zeros((total, w.shape[1]), w.dtype).at[start:start + w.shape[0]].set(w)


def _even_mixer(x, gain_pre, gain_post, w_in, w_out, pool_w, pool_scale, gate_w2, gate_b, gla_norm, batch):
    d = x.shape[1]
    o0 = POOL_WIDTH
    o1 = o0 + GLA_QK
    o2 = o1 + GLA_QK
    o3 = o2 + GLA_WIDTH
    o4 = o3 + GLA_WIDTH
    wb = w_in.astype(BF16)
    w_u = wb[:, :o0][None]
    w_z = jnp.concatenate([_heads(wb[:, o0:o1], GLA_HEADS, GLA_DK), _heads(wb[:, o1:o2], GLA_HEADS, GLA_DK),
                           _heads(wb[:, o2:o3], GLA_HEADS, GLA_DV), _heads(wb[:, o3:o4], GLA_HEADS, GLA_DV)], 2)
    w_glr = jnp.zeros((d, LANES), BF16).at[:, :GLA_GATE_RANK].set(wb[:, o4:])[None]
    u, glr, z = _proj(x, gain_pre, [w_u, w_glr, w_z], [F32, F32, ACT_DTYPE])
    u, glr = u[0], glr[0]
    a_out = _pool(u, pool_w.astype(BF16), pool_scale[None], batch)
    w2p = _heads(_pad_rows(gate_w2, 0, LANES), GLA_HEADS, GLA_DK)
    o = _gla(z, glr, w2p, gate_b.reshape(GLA_HEADS, 1, GLA_DK), gla_norm[None], batch)
    wo = w_out.astype(BF16)
    return _mix_out(a_out, o, wo[:POOL_WIDTH], wo[POOL_WIDTH:], x, gain_post)


def _odd_mixer(x, gain_pre, gain_post, w_in, w_out, mu, w0, w2, a0, a2, g2, k_k, k_a, r_k, ln_w, ln_b,
               cos2, sin2, batch):
    d = x.shape[1]
    wb = w_in.astype(BF16)
    c3 = 3 * RWKV_WIDTH
    rwkv_in = c3 + RWKV_LR
    w6 = jnp.concatenate([_heads(wb[:, :c3], 3, RWKV_WIDTH), _heads(wb[:, rwkv_in:], 3, DIL_WIDTH)], 0)
    w_lr = jnp.zeros((d, RWKV_LR_PAD), BF16).at[:, :RWKV_LR].set(wb[:, c3:rwkv_in])[None]
    z6, lr = _proj(x, gain_pre, [w6, w_lr], [ACT_DTYPE, F32])
    lr = lr[0]
    mu_rkv = mu[:c3].reshape(3, 1, RWKV_WIDTH)
    mu_lr = jnp.zeros((1, RWKV_LR_PAD), F32).at[0, :RWKV_LR].set(mu[c3:])
    w2p = _pad_rows(w2, 0, LANES).astype(BF16)
    a2p = _pad_rows(a2, RWKV_DECAY_RANK, LANES).astype(BF16)
    g2p = _pad_rows(g2, 0, RWKV_LR_PAD - LANES).astype(BF16)
    c_out = _rwkv(z6, lr, mu_rkv, mu_lr, w0[None], w2p, a0[None], a2p, g2p, k_k[None], k_a[None],
                  ln_w[None], ln_b[None], r_k.reshape(1, RWKV_WIDTH), batch)
    d_out = _dilated(z6, cos2, sin2, batch)
    wo = w_out.astype(BF16)
    return _mix_out(c_out, d_out, wo[:RWKV_WIDTH], wo[RWKV_WIDTH:], x, gain_post)


def _rope_tables(s):
    half = DIL_HEAD // 2
    inv = ROPE_THETA ** (-jnp.arange(half, dtype=F32) / half)
    ang = jnp.arange(s).astype(F32)[:, None] * inv[None, :]
    cos, sin = jnp.cos(ang), jnp.sin(ang)
    return jnp.concatenate([cos, cos], axis=-1), jnp.concatenate([-sin, sin], axis=-1)


def kernel(x, p, norm_mix_pre, norm_mix_post, norm_ffn_pre, norm_ffn_post, ev_w_in, ev_w_out, pool_w, pool_scale, gla_gate_w2, gla_gate_b, gla_norm, od_w_in, od_w_out, rwkv_mu, rwkv_w0, rwkv_w2, rwkv_a0, rwkv_a2, rwkv_g2, rwkv_k_k, rwkv_k_a, rwkv_r_k, rwkv_ln_w, rwkv_ln_b, ffn_up, ffn_down, ple_proj, ple_gate, ple_norm):
    batch, s, d = x.shape
    depth = p.shape[0]
    m = batch * s
    xf = x.reshape(m, d)
    cos2, sin2 = _rope_tables(s)
    up_all, down_all = ffn_up.astype(BF16), ffn_down.astype(BF16)
    gate_all, proj_all = ple_gate.astype(BF16), ple_proj.astype(BF16)
    p_all = p.reshape(depth, m, PLE_DIM)
    for i in range(depth):
        j = i // 2
        if i % 2 == 0:
            xf = _even_mixer(xf, norm_mix_pre[i][None], norm_mix_post[i][None], ev_w_in[j], ev_w_out[j],
                             pool_w[j], pool_scale[j], gla_gate_w2[j], gla_gate_b[j], gla_norm[j], batch)
        else:
            xf = _odd_mixer(xf, norm_mix_pre[i][None], norm_mix_post[i][None], od_w_in[j], od_w_out[j],
                            rwkv_mu[j], rwkv_w0[j], rwkv_w2[j], rwkv_a0[j], rwkv_a2[j], rwkv_g2[j],
                            rwkv_k_k[j], rwkv_k_a[j], rwkv_r_k[j], rwkv_ln_w[j], rwkv_ln_b[j],
                            cos2, sin2, batch)
        xf = _ffn(xf, norm_ffn_pre[i][None], norm_ffn_post[i][None], up_all, down_all, i)
        xf = _ple(xf, ple_norm[i][None], gate_all, p_all, proj_all, i)
    return xf.reshape(batch, s, d)
```
